```python
import math
import jax
import jax.numpy as jnp
from jax import lax
import numpy as np

D_MODEL = 1024
BATCH = 2
SEQ = 8192
DEPTH = 2
DEC_BATCH = 32
DEC_SEQ = 8
PAST_LEN = 16384
PAGE_SIZE = 128

HEAD_DIM = 64
LIN_KEY_DIM = 32
H_RET = 4
H_GLA = 4
H_MOBA = 8
RET_W = H_RET * HEAD_DIM
GLA_W = H_GLA * HEAD_DIM
MOBA_W = H_MOBA * HEAD_DIM
MIX_W = RET_W + GLA_W + MOBA_W
RET_QK_W = H_RET * LIN_KEY_DIM
GLA_QK_W = H_GLA * LIN_KEY_DIM
GLA_RANK = 16
GLA_TAU = 16.0
REC_CHUNK = 16
ROPE_BASE = 10000.0
MOBA_BLOCK = 256
MOBA_TOPK = 3
MOBA_Q_CHUNK = 64
N_BUCKETS = 32
MAX_DISTANCE = 128
D_FF = -(-8 * D_MODEL // (3 * 256)) * 256
RMS_EPS = 1e-6
IN_SEGMENTS = (RET_QK_W, RET_QK_W, RET_W, RET_W, GLA_QK_W, GLA_QK_W, GLA_W, GLA_W, GLA_RANK, MOBA_W, MOBA_W, MOBA_W)
IN_W = sum(IN_SEGMENTS)
IN_SPLITS = tuple(sum(IN_SEGMENTS[:i + 1]) for i in range(len(IN_SEGMENTS) - 1))

kernel_name = 'hymba_ret_gla_moba_decoder'


def rmsnorm(x, g):
    xf = x.astype(jnp.float32)
    y = xf * lax.rsqrt(jnp.mean(xf * xf, axis=-1, keepdims=True) + RMS_EPS)
    return (y * g.astype(jnp.float32)).astype(x.dtype)


def rotary(x, pos):
    half = x.shape[-1] // 2
    inv = 1.0 / (ROPE_BASE ** jnp.linspace(0.0, 1.0, half, dtype=jnp.float32))
    ang = pos.astype(jnp.float32)[:, None] * inv[None, :]
    cos = jnp.cos(ang)[None, :, None, :]
    sin = jnp.sin(ang)[None, :, None, :]
    xf = x.astype(jnp.float32)
    x1, x2 = xf[..., :half], xf[..., half:]
    return jnp.concatenate([x1 * cos - x2 * sin, x1 * sin + x2 * cos], axis=-1).astype(x.dtype)


def ret_log_decay():
    return jnp.log(1.0 - 2.0 ** (-5.0 - jnp.arange(H_RET, dtype=jnp.float32)))


def gated_linear_recurrence(q, k, v, log_g, s0):
    B, T, H, dk = q.shape
    dv = v.shape[-1]
    out_dtype = v.dtype
    c = math.gcd(T, REC_CHUNK)
    n = T // c

    def chunks(a):
        return a.astype(jnp.float32).reshape(B, n, c, H, a.shape[-1])

    qf, kf, vf = chunks(q), chunks(k), chunks(v)
    b = jnp.cumsum(chunks(log_g), axis=2)
    b_last = b[:, :, -1:]
    q_dec = qf * jnp.exp(b)
    k_inc = kf * jnp.exp(-b)
    scores = jnp.einsum('bnihd,bnjhd->bnhij', q_dec, k_inc)
    causal = jnp.tril(jnp.ones((c, c), dtype=bool))
    o = jnp.einsum('bnhij,bnjhe->bnihe', jnp.where(causal, scores, 0.0), vf)
    u = jnp.einsum('bnjhd,bnjhe->bnhde', kf * jnp.exp(b_last - b), vf)
    decay = jnp.exp(b_last[:, :, 0])

    def step(s, inp):
        g, du = inp
        return g[..., None] * s + du, s

    s_fin, s_prev = lax.scan(step, s0.astype(jnp.float32), (decay.swapaxes(0, 1), u.swapaxes(0, 1)))
    o = o + jnp.einsum('bnihd,bnhde->bnihe', q_dec, s_prev.swapaxes(0, 1))
    return o.reshape(B, T, H, dv).astype(out_dtype), s_fin.astype(s0.dtype)


def t5_bucket(dist):
    n = jnp.maximum(dist, 0)
    max_exact = N_BUCKETS // 2
    nf = jnp.maximum(n, 1).astype(jnp.float32)
    large = max_exact + (jnp.log(nf / max_exact) / math.log(MAX_DISTANCE / max_exact) * (N_BUCKETS - max_exact)).astype(jnp.int32)
    large = jnp.minimum(large, N_BUCKETS - 1)
    return jnp.where(n < max_exact, n, large)


def join_keys(past, new):
    parts = [new] if past is None else [past, new]
    length = sum(p.shape[1] for p in parts)
    pad = (-length) % MOBA_BLOCK
    if pad:
        parts.append(jnp.zeros((new.shape[0], pad) + new.shape[2:], new.dtype))
    return jnp.concatenate(parts, axis=1)


def moba_attention(q, k_full, v_full, q_pos, rel_bias):
    B, L, H, D = k_full.shape
    nb = L // MOBA_BLOCK
    kb = k_full.reshape(B, nb, MOBA_BLOCK, H, D)
    vb = v_full.reshape(B, nb, MOBA_BLOCK, H, D)
    k_mean = jnp.mean(kb.astype(jnp.float32), axis=2)
    n_sel = min(MOBA_TOPK, nb)
    T = q.shape[1]
    qc = math.gcd(T, MOBA_Q_CHUNK)
    n_chunks = T // qc
    b_idx = jnp.arange(B)[:, None, None, None]
    h_idx = jnp.arange(H)[None, :, None, None]
    offs = jnp.arange(MOBA_BLOCK, dtype=jnp.int32)
    scale = HEAD_DIM ** -0.5
    bias_tab = rel_bias.astype(jnp.float32)

    def attend_chunk(args):
        qi, pi = args
        own = pi // MOBA_BLOCK
        gate = jnp.einsum('bqhd,bnhd->bhqn', qi.astype(jnp.float32), k_mean)
        past = jnp.arange(nb)[None, :] < own[:, None]
        gate = jnp.where(past[None, None], gate, -jnp.inf)
        _, sel = lax.top_k(gate, n_sel)
        sel_ok = sel < own[None, None, :, None]
        own_b = jnp.broadcast_to(own[None, None, :, None], (B, H, qc, 1))
        blocks = jnp.concatenate([sel, own_b], axis=-1).astype(jnp.int32)
        ok = jnp.concatenate([sel_ok, jnp.ones((B, H, qc, 1), dtype=bool)], axis=-1)
        kg = kb[b_idx, blocks, :, h_idx, :]
        vg = vb[b_idx, blocks, :, h_idx, :]
        key_pos = blocks[..., None] * MOBA_BLOCK + offs
        dist = pi[None, None, :, None, None] - key_pos
        valid = ok[..., None] & (dist >= 0)
        bias = bias_tab[t5_bucket(dist), h_idx[..., None]]
        logits = jnp.einsum('bqhd,bhqskd->bhqsk', qi, kg).astype(jnp.float32) * scale + bias
        logits = jnp.where(valid, logits, -jnp.inf)
        s = blocks.shape[-1]
        p = jax.nn.softmax(logits.reshape(B, H, qc, s * MOBA_BLOCK), axis=-1).reshape(B, H, qc, s, MOBA_BLOCK)
        return jnp.einsum('bhqsk,bhqskd->bqhd', p.astype(vg.dtype), vg)

    qs = q.reshape(B, n_chunks, qc, H, D).swapaxes(0, 1)
    ps = q_pos.reshape(n_chunks, qc)
    out = lax.map(attend_chunk, (qs, ps))
    return out.swapaxes(0, 1).reshape(B, T, H * D)


def decoder_layer(x, pos, s_ret0, s_gla0, k_past, v_past, w_in, w_out, gla_w2, gla_b, ret_g, gla_g,
                  n_attn_pre, n_attn_post, n_ffn_pre, n_ffn_post, w_g, w_u, w_d, rel_bias):
    B, T, _ = x.shape
    h = rmsnorm(x, n_attn_pre)
    (rq, rk, rv, rg, gq, gk, gv, gg, ga, mq, mk, mv) = jnp.split(h @ w_in, IN_SPLITS, axis=-1)
    rq = rotary(rq.reshape(B, T, H_RET, LIN_KEY_DIM), pos)
    rk = rotary(rk.reshape(B, T, H_RET, LIN_KEY_DIM), pos) * (LIN_KEY_DIM ** -0.5)
    ret_lg = jnp.broadcast_to(ret_log_decay()[None, None, :, None], (B, T, H_RET, LIN_KEY_DIM))
    o_ret, s_ret = gated_linear_recurrence(rq, rk, rv.reshape(B, T, H_RET, HEAD_DIM), ret_lg, s_ret0)
    o_ret = rmsnorm(o_ret, ret_g).reshape(B, T, RET_W) * jax.nn.silu(rg)
    gla_lg = (jax.nn.log_sigmoid((ga @ gla_w2 + gla_b).astype(jnp.float32)) / GLA_TAU).reshape(B, T, H_GLA, LIN_KEY_DIM)
    o_gla, s_gla = gated_linear_recurrence(gq.reshape(B, T, H_GLA, LIN_KEY_DIM) * (LIN_KEY_DIM ** -0.5),
                                           gk.reshape(B, T, H_GLA, LIN_KEY_DIM),
                                           gv.reshape(B, T, H_GLA, HEAD_DIM), gla_lg, s_gla0)
    o_gla = rmsnorm(o_gla, gla_g).reshape(B, T, GLA_W) * jax.nn.silu(gg)
    mq = mq.reshape(B, T, H_MOBA, HEAD_DIM)
    mk = mk.reshape(B, T, H_MOBA, HEAD_DIM)
    mv = mv.reshape(B, T, H_MOBA, HEAD_DIM)
    o_moba = moba_attention(mq, join_keys(k_past, mk), join_keys(v_past, mv), pos, rel_bias)
    mix = jnp.concatenate([o_ret, o_gla, o_moba], axis=-1) @ w_out
    x = x + rmsnorm(mix, n_attn_post)
    h = rmsnorm(x, n_ffn_pre)
    f = (jax.nn.silu(h @ w_g) * (h @ w_u)) @ w_d
    x = x + rmsnorm(f, n_ffn_post)
    return x, mk, mv, s_ret, s_gla


def setup_inputs(seed: int = 0) -> dict:
    key = jax.random.key(seed)
    ks = jax.random.split(key, 24)
    f32 = jnp.float32
    n_pages = PAST_LEN // PAGE_SIZE
    n_used = DEC_BATCH * n_pages
    n_phys = n_used + max(1, n_used // 4)

    def nrm(k, shape, scale):
        return jax.random.normal(k, shape, f32) * scale

    def gain(k, shape):
        return 1.0 + nrm(k, shape, 0.02)

    page_table = jax.random.permutation(ks[0], n_phys)[:n_used].reshape(DEC_BATCH, n_pages).astype(jnp.int32)
    return {
        'x_prompt': nrm(ks[1], (BATCH, SEQ, D_MODEL), 1.0),
        'x_sample': nrm(ks[2], (DEC_BATCH, DEC_SEQ, D_MODEL), 1.0),
        'cache_k': nrm(ks[3], (DEPTH, n_phys, PAGE_SIZE, H_MOBA, HEAD_DIM), 1.0),
        'cache_v': nrm(ks[4], (DEPTH, n_phys, PAGE_SIZE, H_MOBA, HEAD_DIM), 1.0),
        'state_ret': nrm(ks[5], (DEPTH, DEC_BATCH, H_RET, LIN_KEY_DIM, HEAD_DIM), 0.5),
        'state_gla': nrm(ks[6], (DEPTH, DEC_BATCH, H_GLA, LIN_KEY_DIM, HEAD_DIM), 0.5),
        'page_table': page_table,
        'w_in': nrm(ks[7], (DEPTH, D_MODEL, IN_W), D_MODEL ** -0.5),
        'w_out': nrm(ks[8], (DEPTH, MIX_W, D_MODEL), MIX_W ** -0.5),
        'gla_gate_w2': nrm(ks[9], (DEPTH, GLA_RANK, GLA_QK_W), GLA_RANK ** -0.5),
        'gla_gate_b': nrm(ks[10], (DEPTH, GLA_QK_W), 0.1),
        'ret_norm_g': gain(ks[11], (DEPTH, HEAD_DIM)),
        'gla_norm_g': gain(ks[12], (DEPTH, HEAD_DIM)),
        'norm_attn_pre': gain(ks[13], (DEPTH, D_MODEL)),
        'norm_attn_post': gain(ks[14], (DEPTH, D_MODEL)),
        'norm_ffn_pre': gain(ks[15], (DEPTH, D_MODEL)),
        'norm_ffn_post': gain(ks[16], (DEPTH, D_MODEL)),
        'w_ffn_gate': nrm(ks[17], (DEPTH, D_MODEL, D_FF), D_MODEL ** -0.5),
        'w_ffn_up': nrm(ks[18], (DEPTH, D_MODEL, D_FF), D_MODEL ** -0.5),
        'w_ffn_down': nrm(ks[19], (DEPTH, D_FF, D_MODEL), D_FF ** -0.5),
        'rel_bias': nrm(ks[20], (N_BUCKETS, H_MOBA), 0.5),
    }


def reference(x_prompt, x_sample, cache_k, cache_v, state_ret, state_gla, page_table, w_in, w_out,
              gla_gate_w2, gla_gate_b, ret_norm_g, gla_norm_g, norm_attn_pre, norm_attn_post,
              norm_ffn_pre, norm_ffn_post, w_ffn_gate, w_ffn_up, w_ffn_down, rel_bias):
    B, T_p, _ = x_prompt.shape
    DB, T_s, _ = x_sample.shape
    n_pages = page_table.shape[1]
    past_len = n_pages * cache_k.shape[2]
    pos_p = jnp.arange(T_p, dtype=jnp.int32)
    pos_s = past_len + jnp.arange(T_s, dtype=jnp.int32)
    zero_ret = jnp.zeros((B, H_RET, LIN_KEY_DIM, HEAD_DIM), x_prompt.dtype)
    zero_gla = jnp.zeros((B, H_GLA, LIN_KEY_DIM, HEAD_DIM), x_prompt.dtype)
    hp, hs = x_prompt, x_sample
    kp_l, vp_l, rp_l, gp_l, ks_l, vs_l, rs_l, gs_l = [], [], [], [], [], [], [], []
    for l in range(DEPTH):
        hp, kp, vp, rp, gp = decoder_layer(
            hp, pos_p, zero_ret, zero_gla, None, None, w_in[l], w_out[l], gla_gate_w2[l], gla_gate_b[l],
            ret_norm_g[l], gla_norm_g[l], norm_attn_pre[l], norm_attn_post[l], norm_ffn_pre[l],
            norm_ffn_post[l], w_ffn_gate[l], w_ffn_up[l], w_ffn_down[l], rel_bias)
        k_past = cache_k[l][page_table].reshape(DB, past_len, H_MOBA, HEAD_DIM)
        v_past = cache_v[l][page_table].reshape(DB, past_len, H_MOBA, HEAD_DIM)
        hs, ksn, vsn, rsn, gsn = decoder_layer(
            hs, pos_s, state_ret[l], state_gla[l], k_past, v_past, w_in[l], w_out[l], gla_gate_w2[l],
            gla_gate_b[l], ret_norm_g[l], gla_norm_g[l], norm_attn_pre[l], norm_attn_post[l],
            norm_ffn_pre[l], norm_ffn_post[l], w_ffn_gate[l], w_ffn_up[l], w_ffn_down[l], rel_bias)
        kp_l.append(kp)
        vp_l.append(vp)
        rp_l.append(rp)
        gp_l.append(gp)
        ks_l.append(ksn)
        vs_l.append(vsn)
        rs_l.append(rsn)
        gs_l.append(gsn)
    return (hp, hs, jnp.stack(kp_l), jnp.stack(vp_l), jnp.stack(rp_l), jnp.stack(gp_l),
            jnp.stack(ks_l), jnp.stack(vs_l), jnp.stack(rs_l), jnp.stack(gs_l))
```

```python
import functools
import math

import jax
import jax.numpy as jnp
import numpy as np
from jax import lax
from jax.experimental import pallas as pl
from jax.experimental.pallas import tpu as pltpu

F32 = jnp.float32
BF16 = jnp.bfloat16

HEAD_DIM = 64
LIN_KEY_DIM = 32
N_LIN_HEADS = 4
H_MOBA = 8
LIN_QK_W = N_LIN_HEADS * LIN_KEY_DIM
LIN_V_W = N_LIN_HEADS * HEAD_DIM
MOBA_W = H_MOBA * HEAD_DIM
GLA_RANK = 16
GLA_TAU = 16.0
REC_CHUNK = 16
ROPE_BASE = 10000.0
MOBA_BLOCK = 256
MOBA_TOPK = 3
N_BUCKETS = 32
MAX_DISTANCE = 128
RMS_EPS = 1e-6
LIN_TOK_W = 2 * (2 * LIN_QK_W + 2 * LIN_V_W)
GA_PAD = 128
TOK_W = LIN_TOK_W + GA_PAD

VMEM_LIMIT_BYTES = 56 * 1024 * 1024
MASK_NEG = -1e30

_NT = (((1,), (1,)), ((), ()))
_TN = (((0,), (0,)), ((), ()))


def _cparams(*sem):
    return pltpu.CompilerParams(dimension_semantics=sem, vmem_limit_bytes=VMEM_LIMIT_BYTES)


def _rms(x, g):
    return x * lax.rsqrt(jnp.mean(x * x, axis=-1, keepdims=True) + RMS_EPS) * g


def _split3(x):
    hi = x.astype(BF16)
    r = x - hi.astype(F32)
    mid = r.astype(BF16)
    lo = (r - mid.astype(F32)).astype(BF16)
    return hi, mid, lo


def _dot(a, b):
    return jnp.dot(a, b, preferred_element_type=F32)


def _ldot3(a_bf16, x):
    hi, mid, lo = _split3(x)
    return _dot(a_bf16, hi) + _dot(a_bf16, mid) + _dot(a_bf16, lo)


def _rdot3(x, b_bf16):
    hi, mid, lo = _split3(x)
    return _dot(hi, b_bf16) + _dot(mid, b_bf16) + _dot(lo, b_bf16)


def _inproj_kernel(x_ref, g_ref, wtok_ref, wq_ref, wk_ref, wv_ref, tok_ref, q_ref, k_ref, v_ref):
    hb = _rms(x_ref[0], g_ref[...]).astype(BF16)
    tok_ref[0] = _dot(hb, wtok_ref[...])
    q_ref[0] = lax.dot_general(wq_ref[...], hb, _NT, preferred_element_type=F32)
    k_ref[0] = lax.dot_general(wk_ref[...], hb, _NT, preferred_element_type=F32)
    v_ref[0] = lax.dot_general(wv_ref[...], hb, _NT, preferred_element_type=F32)


def _inproj(x, g, wtok, wq_t, wk_t, wv_t, tm):
    nb, t, d = x.shape
    full = lambda s: pl.BlockSpec(s, lambda b, i: (0,) * len(s))
    chn = pl.BlockSpec((1, MOBA_W, tm), lambda b, i: (b, 0, i))
    return pl.pallas_call(
        _inproj_kernel,
        grid=(nb, t // tm),
        in_specs=[pl.BlockSpec((1, tm, d), lambda b, i: (b, i, 0)), full((1, d)), full((d, TOK_W)),
                  full((MOBA_W, d)), full((MOBA_W, d)), full((MOBA_W, d))],
        out_specs=[pl.BlockSpec((1, tm, TOK_W), lambda b, i: (b, i, 0)), chn, chn, chn],
        out_shape=[jax.ShapeDtypeStruct((nb, t, TOK_W), F32)] + [jax.ShapeDtypeStruct((nb, MOBA_W, t), F32)] * 3,
        compiler_params=_cparams("parallel", "parallel"),
        name="inproj",
    )(x, g, wtok, wq_t, wk_t, wv_t)


def _linrec_kernel(tok_ref, cos_ref, sin_ref, w2_ref, gb_ref, rdec_ref, rgain_ref, ggain_ref, s0_ref,
                   o_ref, sfin_ref, s_scr, *, tb, c):
    t = pl.program_id(1)

    @pl.when(t == 0)
    def _():
        s_scr[...] = s0_ref[0]

    n_sub = tb // c
    shift = int(math.log2(c))
    row = lax.broadcasted_iota(jnp.int32, (tb, tb), 0)
    col = lax.broadcasted_iota(jnp.int32, (tb, tb), 1)
    same = (row >> shift) == (col >> shift)
    tril = same & (col <= row)
    tril_b = jnp.where(tril, 1.0, 0.0).astype(BF16)
    same_b = jnp.where(same, 1.0, 0.0).astype(BF16)
    head_qk = lax.broadcasted_iota(jnp.int32, (1, LIN_QK_W), 1) // LIN_KEY_DIM
    head_v = lax.broadcasted_iota(jnp.int32, (1, LIN_V_W), 1) // HEAD_DIM
    blockdiag = (lax.broadcasted_iota(jnp.int32, (LIN_QK_W, LIN_V_W), 0) // LIN_KEY_DIM
                 == lax.broadcasted_iota(jnp.int32, (LIN_QK_W, LIN_V_W), 1) // HEAD_DIM)
    gi = lax.broadcasted_iota(jnp.int32, (LIN_V_W, LIN_V_W), 0) // HEAD_DIM
    gj = lax.broadcasted_iota(jnp.int32, (LIN_V_W, LIN_V_W), 1) // HEAD_DIM
    head_mean = jnp.where(gi == gj, 1.0 / HEAD_DIM, 0.0).astype(BF16)
    first_half = (lax.broadcasted_iota(jnp.int32, (1, LIN_QK_W), 1) % LIN_KEY_DIM) < LIN_KEY_DIM // 2
    cos = cos_ref[...]
    sin = sin_ref[...]
    qk_scale = LIN_KEY_DIM ** -0.5

    def rotary(x):
        rot = jnp.where(first_half, -pltpu.roll(x, LIN_QK_W - LIN_KEY_DIM // 2, 1), pltpu.roll(x, LIN_KEY_DIM // 2, 1))
        return x * cos + rot * sin

    def group(g_idx, q, k, v, lg, gate, gain):
        b = _ldot3(tril_b, lg)
        b_end = _ldot3(same_b, lg)
        q_dec = q * jnp.exp(b)
        k_inc = (k * jnp.exp(-b)).astype(BF16)
        k_dec = k * jnp.exp(b_end - b)
        b_end_t = b_end.T
        vb = v.astype(BF16)
        o = jnp.zeros((tb, LIN_V_W), F32)
        for h in range(N_LIN_HEADS):
            qh = jnp.where(head_qk == h, q_dec, 0.0).astype(BF16)
            sc = lax.dot_general(qh, k_inc, _NT, preferred_element_type=F32)
            p = jnp.where(tril, sc, 0.0).astype(BF16)
            o = o + jnp.where(head_v == h, _dot(p, vb), 0.0)
        s = s_scr[g_idx]
        inter = []
        for i in range(n_sub):
            lo, hi = i * c, (i + 1) * c
            inter.append(_dot(q_dec[lo:hi].astype(BF16), s.astype(BF16)))
            u = lax.dot_general(k_dec[lo:hi].astype(BF16), v[lo:hi].astype(BF16), _TN, preferred_element_type=F32)
            s = jnp.exp(b_end_t[:, lo:lo + 1]) * s + jnp.where(blockdiag, u, 0.0)
        s_scr[g_idx] = s
        o = o + (inter[0] if n_sub == 1 else jnp.concatenate(inter, axis=0))
        ms = _rdot3(o * o, head_mean)
        on = o * lax.rsqrt(ms + RMS_EPS) * gain
        return on * (gate / (1.0 + jnp.exp(-gate)))

    a = LIN_QK_W
    w = LIN_V_W
    rq = rotary(tok_ref[0, :, 0:a])
    rk = rotary(tok_ref[0, :, a:2 * a]) * qk_scale
    rv = tok_ref[0, :, 2 * a:2 * a + w]
    rgate = tok_ref[0, :, 2 * a + w:2 * a + 2 * w]
    ret_lg = jnp.broadcast_to(rdec_ref[...], (tb, LIN_QK_W))
    o_ref[0, :, 0:w] = group(0, rq, rk, rv, ret_lg, rgate, rgain_ref[...]).astype(o_ref.dtype)

    base = 2 * a + 2 * w
    gq = tok_ref[0, :, base:base + a] * qk_scale
    gk = tok_ref[0, :, base + a:base + 2 * a]
    gv = tok_ref[0, :, base + 2 * a:base + 2 * a + w]
    ggate = tok_ref[0, :, base + 2 * a + w:base + 2 * a + 2 * w]
    ga = tok_ref[0, :, LIN_TOK_W:TOK_W]
    z = _dot(ga.astype(BF16), w2_ref[...]) + gb_ref[...]
    gla_lg = (jnp.minimum(z, 0.0) - jnp.log(1.0 + jnp.exp(-jnp.abs(z)))) * (1.0 / GLA_TAU)
    o_ref[0, :, w:2 * w] = group(1, gq, gk, gv, gla_lg, ggate, ggain_ref[...]).astype(o_ref.dtype)

    @pl.when(t == pl.num_programs(1) - 1)
    def _():
        sfin_ref[0] = s_scr[...]


def _linrec(tok, cos, sin, w2p, gb, rdec, rgain, ggain, s0, tb, c):
    nb, t, _ = tok.shape
    full = lambda s: pl.BlockSpec(s, lambda b, i: (0,) * len(s))
    st = pl.BlockSpec((1, 2, LIN_QK_W, LIN_V_W), lambda b, i: (b, 0, 0, 0))
    return pl.pallas_call(
        functools.partial(_linrec_kernel, tb=tb, c=c),
        grid=(nb, t // tb),
        in_specs=[pl.BlockSpec((1, tb, TOK_W), lambda b, i: (b, i, 0)),
                  pl.BlockSpec((tb, LIN_QK_W), lambda b, i: (i, 0)),
                  pl.BlockSpec((tb, LIN_QK_W), lambda b, i: (i, 0)),
                  full((GA_PAD, LIN_QK_W)), full((1, LIN_QK_W)), full((1, LIN_QK_W)),
                  full((1, LIN_V_W)), full((1, LIN_V_W)), st],
        out_specs=[pl.BlockSpec((1, tb, 2 * LIN_V_W), lambda b, i: (b, i, 0)), st],
        out_shape=[jax.ShapeDtypeStruct((nb, t, 2 * LIN_V_W), BF16),
                   jax.ShapeDtypeStruct((nb, 2, LIN_QK_W, LIN_V_W), F32)],
        scratch_shapes=[pltpu.VMEM((2, LIN_QK_W, LIN_V_W), F32)],
        compiler_params=_cparams("parallel", "arbitrary"),
        name="linrec",
    )(tok, cos, sin, w2p, gb, rdec, rgain, ggain, s0)


def _select_topk(gate_t, n_valid, nbp, width):
    rown = lax.broadcasted_iota(jnp.int32, (nbp, width), 0)
    rown_f = rown.astype(F32)
    valid = rown < n_valid
    g = jnp.where(valid, gate_t, -jnp.inf)
    sel = jnp.zeros((nbp, width), F32)
    for _ in range(MOBA_TOPK):
        mx = jnp.max(g, axis=0, keepdims=True)
        idx = jnp.min(jnp.where(g == mx, rown_f, float(nbp)), axis=0, keepdims=True)
        pick = rown_f == idx
        sel = jnp.where(pick, jnp.where(valid, 1.0, 0.0), sel)
        g = jnp.where(pick, -jnp.inf, g)
    return sel, rown


def _moba_prompt_kernel(far_ref, q_ref, k_ref, v_ref, down_ref, dprev_ref, o_ref,
                        kaug_scr, vb_scr, kmean_scr, *, nb, nbp):
    hp = pl.program_id(1)
    qi = pl.program_id(2)
    blk = MOBA_BLOCK
    n_tail = 2 * HEAD_DIM - HEAD_DIM - nbp

    @pl.when(qi == 0)
    def _():
        lane = lax.broadcasted_iota(jnp.int32, (blk, HEAD_DIM), 1)
        for hh in range(2):
            rows = slice(hh * HEAD_DIM, (hh + 1) * HEAD_DIM)
            vb_scr[hh] = v_ref[0, rows, :].astype(BF16)
            kmean_scr[hh] = jnp.zeros((nbp, HEAD_DIM), F32)

            def fill(n, carry):
                off = pl.multiple_of(n * blk, blk)
                kt = k_ref[0, rows, pl.ds(off, blk)]
                kb = kt.T
                kmean_scr[hh, pl.ds(n, 1), :] = jnp.mean(kb, axis=0, keepdims=True)
                extra = jnp.where(lane == n, 1.0, jnp.where((lane == nbp) | (lane == nbp + 1), 1.0, 0.0))
                kaug_scr[hh, pl.ds(off, blk), :] = jnp.concatenate([kb, extra], axis=1).astype(BF16)
                return carry

            lax.fori_loop(0, nb, fill, 0)

    row = lax.broadcasted_iota(jnp.int32, (blk, blk), 0)
    col = lax.broadcasted_iota(jnp.int32, (blk, blk), 1)
    causal = row <= col
    tail_row = lax.broadcasted_iota(jnp.int32, (n_tail, blk), 0)
    jp = jnp.maximum(qi - 1, 0)

    for hh in range(2):
        head = hp * 2 + hh
        rows = slice(hh * HEAD_DIM, (hh + 1) * HEAD_DIM)
        q_t = q_ref[0, rows, :]
        gate_t = jnp.dot(kmean_scr[hh], q_t, preferred_element_type=F32, precision=lax.Precision.HIGHEST)
        sel, rown = _select_topk(gate_t, qi, nbp, blk)
        mask_t = jnp.where((sel > 0.0) | (rown == qi), 0.0, MASK_NEG)
        tail = jnp.where(tail_row == 0, far_ref[2 * head], jnp.where(tail_row == 1, far_ref[2 * head + 1], 0.0))
        q_aug = jnp.concatenate([q_t * (HEAD_DIM ** -0.5), mask_t, tail], axis=0).astype(BF16)

        def scores(j):
            off = pl.multiple_of(j * blk, blk)
            return _dot(kaug_scr[hh, pl.ds(off, blk), :], q_aug), vb_scr[hh, :, pl.ds(off, blk)]

        s, vblk = scores(qi)
        s = jnp.where(causal, s + down_ref[hh], MASK_NEG)
        m = jnp.max(s, axis=0, keepdims=True)
        p = jnp.exp(s - m)
        l = jnp.sum(p, axis=0, keepdims=True)
        acc = _dot(vblk, p.astype(BF16))

        def online(s, vblk, m, l, acc):
            m_new = jnp.maximum(m, jnp.max(s, axis=0, keepdims=True))
            alpha = jnp.exp(m - m_new)
            p = jnp.exp(s - m_new)
            l = l * alpha + jnp.sum(p, axis=0, keepdims=True)
            acc = acc * alpha + _dot(vblk, p.astype(BF16))
            return m_new, l, acc

        s, vblk = scores(jp)
        s = jnp.where(qi > 0, s + dprev_ref[hh], MASK_NEG)
        m, l, acc = online(s, vblk, m, l, acc)

        def body(j, carry):
            s, vblk = scores(j)
            return online(s, vblk, *carry)

        m, l, acc = lax.fori_loop(0, jp, body, (m, l, acc))
        o_ref[0, :, hh * HEAD_DIM:(hh + 1) * HEAD_DIM] = (acc / l).T.astype(o_ref.dtype)


def _moba_prompt(q_t, k_t, v_t, far, d_own_t, d_prev_t):
    nbat, _, t = q_t.shape
    nb = t // MOBA_BLOCK
    nbp = max(8, -(-nb // 8) * 8)
    assert t % MOBA_BLOCK == 0 and HEAD_DIM + nbp + 2 <= 2 * HEAD_DIM
    tab = pl.BlockSpec((2, MOBA_BLOCK, MOBA_BLOCK), lambda b, hp, qi, far: (hp, 0, 0))
    kv = pl.BlockSpec((1, 2 * HEAD_DIM, t), lambda b, hp, qi, far: (b, hp, 0))
    return pl.pallas_call(
        functools.partial(_moba_prompt_kernel, nb=nb, nbp=nbp),
        grid_spec=pltpu.PrefetchScalarGridSpec(
            num_scalar_prefetch=1,
            grid=(nbat, H_MOBA // 2, nb),
            in_specs=[pl.BlockSpec((1, 2 * HEAD_DIM, MOBA_BLOCK), lambda b, hp, qi, far: (b, hp, qi)), kv, kv, tab, tab],
            out_specs=pl.BlockSpec((1, MOBA_BLOCK, 2 * HEAD_DIM), lambda b, hp, qi, far: (b, qi, hp)),
            scratch_shapes=[pltpu.VMEM((2, t, 2 * HEAD_DIM), BF16), pltpu.VMEM((2, HEAD_DIM, t), BF16),
                            pltpu.VMEM((2, nbp, HEAD_DIM), F32)],
        ),
        out_shape=jax.ShapeDtypeStruct((nbat, t, MOBA_W), BF16),
        compiler_params=_cparams("parallel", "parallel", "arbitrary"),
        name="moba_prompt",
    )(far, q_t, k_t, v_t, d_own_t, d_prev_t)


N_PAGE_BUF = 8


def _sample_select_kernel(pt_ref, q_ref, cache_ref, sel_ref, buf, sem, km_scr, *, layer, n_pages, pages_per_block):
    b = pl.program_id(0)
    nblk = n_pages // pages_per_block
    page = cache_ref.shape[-1]

    def copy(p, slot):
        return pltpu.make_async_copy(cache_ref.at[layer, pt_ref[b, p]], buf.at[slot], sem.at[slot])

    for p in range(N_PAGE_BUF):
        copy(p, p).start()
    km_scr[...] = jnp.zeros(km_scr.shape, F32)
    lane = lax.broadcasted_iota(jnp.int32, km_scr.shape, 1)

    def block(j, carry):
        tot = jnp.zeros((H_MOBA * HEAD_DIM, 1), F32)
        for u in range(pages_per_block):
            p = j * pages_per_block + u
            slot = p % N_PAGE_BUF
            copy(p, slot).wait()
            tot = tot + jnp.sum(buf[slot].reshape(H_MOBA * HEAD_DIM, page), axis=1, keepdims=True)

            @pl.when(p + N_PAGE_BUF < n_pages)
            def _():
                copy(p + N_PAGE_BUF, slot).start()

        km_scr[...] = jnp.where(lane == j, tot * (1.0 / (pages_per_block * page)), km_scr[...])
        return carry

    lax.fori_loop(0, nblk, block, 0)

    km = km_scr[...]
    n_t = q_ref.shape[-1]
    out_lane = lax.broadcasted_iota(jnp.int32, (H_MOBA, 128), 1)
    gl = lax.broadcasted_iota(jnp.int32, (H_MOBA, km.shape[1]), 1)
    gl_f = gl.astype(F32)
    out = jnp.zeros((H_MOBA, 128), jnp.int32)
    for t in range(n_t):
        gate = (q_ref[0, :, t:t + 1] * km).reshape(H_MOBA, HEAD_DIM, km.shape[1]).sum(axis=1)
        g = jnp.where(gl < nblk, gate, -jnp.inf)
        for r in range(MOBA_TOPK):
            mx = jnp.max(g, axis=1, keepdims=True)
            idx = jnp.min(jnp.where(g == mx, gl_f, float(km.shape[1])), axis=1, keepdims=True)
            out = jnp.where(out_lane == t * 4 + r, idx.astype(jnp.int32), out)
            g = jnp.where(gl_f == idx, -jnp.inf, g)
    sel_ref[0] = out


def _sample_select(page_table, q_bt, cache_t, layer):
    nbat, n_pages = page_table.shape
    page = cache_t.shape[-1]
    ppb = MOBA_BLOCK // page
    nblk = n_pages // ppb
    nblk_pad = -(-nblk // 128) * 128
    assert n_pages % ppb == 0 and n_pages >= N_PAGE_BUF and q_bt.shape[-1] * 4 <= 128
    return pl.pallas_call(
        functools.partial(_sample_select_kernel, layer=layer, n_pages=n_pages, pages_per_block=ppb),
        grid_spec=pltpu.PrefetchScalarGridSpec(
            num_scalar_prefetch=1,
            grid=(nbat,),
            in_specs=[pl.BlockSpec((1, MOBA_W, q_bt.shape[-1]), lambda b, pt: (b, 0, 0)),
                      pl.BlockSpec(memory_space=pl.ANY)],
            out_specs=pl.BlockSpec((1, H_MOBA, 128), lambda b, pt: (b, 0, 0)),
            scratch_shapes=[pltpu.VMEM((N_PAGE_BUF, H_MOBA, HEAD_DIM, page), F32),
                            pltpu.SemaphoreType.DMA((N_PAGE_BUF,)),
                            pltpu.VMEM((MOBA_W, nblk_pad), F32)],
        ),
        out_shape=jax.ShapeDtypeStruct((nbat, H_MOBA, 128), jnp.int32),
        compiler_params=_cparams("arbitrary"),
        name="sample_select",
    )(page_table, q_bt, cache_t)


def _sample_attend_kernel(pt_ref, sel_ref, q_ref, kn_ref, vn_ref, bias_ref, ck_ref, cv_ref, o_ref,
                          kbuf, vbuf, sem, *, layer, n_t, pages_per_block, nblk):
    b = pl.program_id(0)
    page = ck_ref.shape[-1]
    n_slot = n_t * MOBA_TOPK

    def copies(h, par):
        out = []
        for t in range(n_t):
            for r in range(MOBA_TOPK):
                j = sel_ref[b, (h * n_t + t) * MOBA_TOPK + r]
                slot = t * MOBA_TOPK + r
                for u in range(pages_per_block):
                    pid = pt_ref[b, j * pages_per_block + u]
                    lanes = pl.ds(u * page, page)
                    out.append(pltpu.make_async_copy(ck_ref.at[layer, pid, h], kbuf.at[par, slot, :, lanes], sem.at[par, 0]))
                    out.append(pltpu.make_async_copy(cv_ref.at[layer, pid, h], vbuf.at[par, slot, :, lanes], sem.at[par, 1]))
        return out

    for cp in copies(0, 0):
        cp.start()
    key_off = lax.broadcasted_iota(jnp.int32, (1, n_t), 1)

    def per_head(h, carry):
        par = h % 2

        @pl.when(h + 1 < H_MOBA)
        def _():
            for cp in copies(h + 1, 1 - par):
                cp.start()

        for cp in copies(h, par):
            cp.wait()
        rows = pl.ds(pl.multiple_of(h * HEAD_DIM, HEAD_DIM), HEAD_DIM)
        qh = q_ref[0, rows, :] * (HEAD_DIM ** -0.5)
        kn = kn_ref[0, rows, :]
        vn = vn_ref[0, rows, :]
        for t in range(n_t):
            qcol = qh[:, t:t + 1]
            s_own = jnp.sum(qcol * kn, axis=0, keepdims=True) + bias_ref[h, n_t + 1 + t:n_t + 2 + t, 0:n_t]
            s_own = jnp.where(key_off <= t, s_own, MASK_NEG)
            ss = []
            for r in range(MOBA_TOPK):
                j = sel_ref[b, (h * n_t + t) * MOBA_TOPK + r]
                s = jnp.sum(qcol * kbuf[par, t * MOBA_TOPK + r], axis=0, keepdims=True)
                ss.append(s + jnp.where(j == nblk - 1, bias_ref[h, t:t + 1, :], bias_ref[h, n_t:n_t + 1, :]))
            m = jnp.max(s_own, axis=1, keepdims=True)
            for s in ss:
                m = jnp.maximum(m, jnp.max(s, axis=1, keepdims=True))
            p_own = jnp.exp(s_own - m)
            l = jnp.sum(p_own, axis=1, keepdims=True)
            acc = jnp.sum(p_own * vn, axis=1, keepdims=True)
            for r, s in enumerate(ss):
                p = jnp.exp(s - m)
                l = l + jnp.sum(p, axis=1, keepdims=True)
                acc = acc + jnp.sum(p * vbuf[par, t * MOBA_TOPK + r], axis=1, keepdims=True)
            o_ref[0, rows, t:t + 1] = acc / l
        return carry

    lax.fori_loop(0, H_MOBA, per_head, 0)


def _sample_attend(page_table, sel, q_bt, kn_bt, vn_bt, bias_rows, cache_kt, cache_vt, layer):
    nbat, n_pages = page_table.shape
    n_t = q_bt.shape[-1]
    page = cache_kt.shape[-1]
    ppb = MOBA_BLOCK // page
    n_slot = n_t * MOBA_TOPK
    new = pl.BlockSpec((1, MOBA_W, n_t), lambda b, pt, sl: (b, 0, 0))
    return pl.pallas_call(
        functools.partial(_sample_attend_kernel, layer=layer, n_t=n_t, pages_per_block=ppb, nblk=n_pages // ppb),
        grid_spec=pltpu.PrefetchScalarGridSpec(
            num_scalar_prefetch=2,
            grid=(nbat,),
            in_specs=[new, new, new,
                      pl.BlockSpec(bias_rows.shape, lambda b, pt, sl: (0, 0, 0)),
                      pl.BlockSpec(memory_space=pl.ANY), pl.BlockSpec(memory_space=pl.ANY)],
            out_specs=new,
            scratch_shapes=[pltpu.VMEM((2, n_slot, HEAD_DIM, MOBA_BLOCK), F32),
                            pltpu.VMEM((2, n_slot, HEAD_DIM, MOBA_BLOCK), F32),
                            pltpu.SemaphoreType.DMA((2, 2))],
        ),
        out_shape=jax.ShapeDtypeStruct((nbat, MOBA_W, n_t), F32),
        compiler_params=_cparams("arbitrary"),
        name="sample_attend",
    )(page_table, sel, q_bt, kn_bt, vn_bt, bias_rows, cache_kt, cache_vt)


def _outproj_kernel(x_ref, olin_ref, omoba_ref, wl_ref, wm_ref, g_ref, y_ref):
    mix = _dot(olin_ref[...], wl_ref[...]) + _dot(omoba_ref[...], wm_ref[...])
    y_ref[...] = x_ref[...] + _rms(mix, g_ref[...])


def _outproj(x, olin, omoba, w_lin, w_moba, g, tm):
    n, d = x.shape
    full = lambda s: pl.BlockSpec(s, lambda i: (0,) * len(s))
    rows = lambda w: pl.BlockSpec((tm, w), lambda i: (i, 0))
    return pl.pallas_call(
        _outproj_kernel,
        grid=(n // tm,),
        in_specs=[rows(d), rows(olin.shape[1]), rows(omoba.shape[1]), full(w_lin.shape), full(w_moba.shape), full((1, d))],
        out_specs=rows(d),
        out_shape=jax.ShapeDtypeStruct((n, d), F32),
        compiler_params=_cparams("parallel"),
        name="outproj",
    )(x, olin, omoba, w_lin, w_moba, g)


def _ffn_kernel(x_ref, gpre_ref, wg_ref, wu_ref, wd_ref, gpost_ref, y_ref, h_scr, acc_scr):
    j = pl.program_id(1)

    @pl.when(j == 0)
    def _():
        h_scr[...] = _rms(x_ref[...], gpre_ref[...]).astype(BF16)
        acc_scr[...] = jnp.zeros(acc_scr.shape, F32)

    h = h_scr[...]
    gate = _dot(h, wg_ref[...])
    up = _dot(h, wu_ref[...])
    act = (gate / (1.0 + jnp.exp(-gate)) * up).astype(BF16)
    acc_scr[...] += _dot(act, wd_ref[...])

    @pl.when(j == pl.num_programs(1) - 1)
    def _():
        y_ref[...] = x_ref[...] + _rms(acc_scr[...], gpost_ref[...])


def _ffn(x, gpre, wg, wu, wd, gpost, tm, fc):
    n, d = x.shape
    dff = wg.shape[1]
    rows = pl.BlockSpec((tm, d), lambda i, j: (i, 0))
    vec = pl.BlockSpec((1, d), lambda i, j: (0, 0))
    return pl.pallas_call(
        _ffn_kernel,
        grid=(n // tm, dff // fc),
        in_specs=[rows, vec, pl.BlockSpec((d, fc), lambda i, j: (0, j)), pl.BlockSpec((d, fc), lambda i, j: (0, j)),
                  pl.BlockSpec((fc, d), lambda i, j: (j, 0)), vec],
        out_specs=rows,
        out_shape=jax.ShapeDtypeStruct((n, d), F32),
        scratch_shapes=[pltpu.VMEM((tm, d), BF16), pltpu.VMEM((tm, d), F32)],
        compiler_params=_cparams("parallel", "arbitrary"),
        name="ffn",
    )(x, gpre, wg, wu, wd, gpost)


def _t5_bucket_np(dist):
    n = np.maximum(dist, 0)
    max_exact = N_BUCKETS // 2
    nf = np.maximum(n, 1).astype(np.float32)
    large = max_exact + (np.log(nf / np.float32(max_exact)) / np.float32(math.log(MAX_DISTANCE / max_exact))
                         * np.float32(N_BUCKETS - max_exact)).astype(np.int32)
    large = np.minimum(large, N_BUCKETS - 1)
    return np.where(n < max_exact, n, large).astype(np.int32)


def _rotary_tables(pos):
    half = LIN_KEY_DIM // 2
    inv = 1.0 / (ROPE_BASE ** jnp.linspace(0.0, 1.0, half, dtype=F32))
    ang = pos.astype(F32)[:, None] * inv[None, :]
    tile = lambda a: jnp.tile(a, (1, 2 * N_LIN_HEADS))
    return tile(jnp.cos(ang)), tile(jnp.sin(ang))


def _blockdiag_state(s):
    eye = jnp.eye(N_LIN_HEADS, dtype=s.dtype)
    return (s[:, :, :, None, :] * eye[None, :, None, :, None]).reshape(s.shape[0], LIN_QK_W, LIN_V_W)


def _head_states(sb):
    x = sb.reshape(sb.shape[0], N_LIN_HEADS, LIN_KEY_DIM, N_LIN_HEADS, HEAD_DIM)
    return jnp.stack([x[:, h, :, h, :] for h in range(N_LIN_HEADS)], axis=1)


def _row_tile(n, want):
    t = min(n, want)
    while n % t:
        t //= 2
    return t


def kernel(x_prompt, x_sample, cache_k, cache_v, state_ret, state_gla, page_table, w_in, w_out, gla_gate_w2, gla_gate_b, ret_norm_g, gla_norm_g, norm_attn_pre, norm_attn_post, norm_ffn_pre, norm_ffn_post, w_ffn_gate, w_ffn_up, w_ffn_down, rel_bias):
    depth = w_in.shape[0]
    nbp_, t_p, d_model = x_prompt.shape
    nbs, t_s, _ = x_sample.shape
    n_pages = page_table.shape[1]
    page = cache_k.shape[2]
    past_len = n_pages * page
    assert past_len % MOBA_BLOCK == 0 and t_s <= MOBA_BLOCK and t_p % MOBA_BLOCK == 0
    dff = w_ffn_gate.shape[2]

    cos_p, sin_p = _rotary_tables(jnp.arange(t_p, dtype=jnp.int32))
    cos_s, sin_s = _rotary_tables(past_len + jnp.arange(t_s, dtype=jnp.int32))
    off = np.arange(MOBA_BLOCK)
    bucket_own_t = _t5_bucket_np(off[None, :] - off[:, None])
    bucket_prev_t = _t5_bucket_np(MOBA_BLOCK + off[None, :] - off[:, None])
    bias_h = rel_bias.astype(F32).T
    far = bias_h[:, N_BUCKETS - 1]
    far_hi = far.astype(BF16).astype(F32)
    far_split = jnp.stack([far_hi, far - far_hi], axis=1).reshape(-1)
    d_own_t = bias_h[:, bucket_own_t] - far[:, None, None]
    d_prev_t = bias_h[:, bucket_prev_t] - far[:, None, None]
    prev_rows = bias_h[:, _t5_bucket_np(MOBA_BLOCK + np.arange(t_s)[:, None] - off[None, :])]
    far_row = jnp.broadcast_to(far[:, None, None], (H_MOBA, 1, MOBA_BLOCK))
    own_rows = bias_h[:, _t5_bucket_np(np.arange(t_s)[:, None] - off[None, :])]
    n_rows = 2 * t_s + 1
    pad_rows = -(-n_rows // 8) * 8 - n_rows
    bias_rows = jnp.concatenate([prev_rows, far_row, own_rows, jnp.zeros((H_MOBA, pad_rows, MOBA_BLOCK), F32)], axis=1)

    rdec = jnp.repeat(jnp.log(1.0 - 2.0 ** (-5.0 - jnp.arange(N_LIN_HEADS, dtype=F32))), LIN_KEY_DIM)[None, :]
    cache_kt = jnp.transpose(cache_k, (0, 1, 3, 4, 2))
    cache_vt = jnp.transpose(cache_v, (0, 1, 3, 4, 2))
    w_in_t = jnp.swapaxes(w_in, 1, 2)

    hp = x_prompt
    hs = x_sample.reshape(1, nbs * t_s, d_model)
    zero_state = jnp.zeros((nbp_, 2, LIN_QK_W, LIN_V_W), F32)
    tm_p = _row_tile(t_p, 512)
    tb_p = _row_tile(t_p, 128)
    c_p = math.gcd(t_p, REC_CHUNK)
    c_s = math.gcd(t_s, REC_CHUNK)
    fc = dff // 2 if (dff // 2) % 128 == 0 else dff
    outs = {k: [] for k in ("kp", "vp", "rp", "gp", "ks", "vs", "rs", "gs")}

    for l in range(depth):
        seg = w_in_t[l]
        wtok = jnp.concatenate([seg[:LIN_TOK_W + GLA_RANK], jnp.zeros((GA_PAD - GLA_RANK, d_model), F32)], axis=0).T.astype(BF16)
        m0 = LIN_TOK_W + GLA_RANK
        wq_t = seg[m0:m0 + MOBA_W].astype(BF16)
        wk_t = seg[m0 + MOBA_W:m0 + 2 * MOBA_W].astype(BF16)
        wv_t = seg[m0 + 2 * MOBA_W:m0 + 3 * MOBA_W].astype(BF16)
        w2p = jnp.concatenate([gla_gate_w2[l], jnp.zeros((GA_PAD - GLA_RANK, LIN_QK_W), F32)], axis=0).astype(BF16)
        gb = gla_gate_b[l][None, :]
        rgain = jnp.tile(ret_norm_g[l], N_LIN_HEADS)[None, :]
        ggain = jnp.tile(gla_norm_g[l], N_LIN_HEADS)[None, :]
        w_o = w_out[l].astype(BF16)
        w_o_lin, w_o_moba = w_o[:2 * LIN_V_W], w_o[2 * LIN_V_W:]
        wg, wu, wd = w_ffn_gate[l].astype(BF16), w_ffn_up[l].astype(BF16), w_ffn_down[l].astype(BF16)
        g_pre, g_post = norm_attn_pre[l][None, :], norm_attn_post[l][None, :]
        f_pre, f_post = norm_ffn_pre[l][None, :], norm_ffn_post[l][None, :]

        tok, q_t, k_t, v_t = _inproj(hp, g_pre, wtok, wq_t, wk_t, wv_t, tm_p)
        olin, sfin = _linrec(tok, cos_p, sin_p, w2p, gb, rdec, rgain, ggain, zero_state, tb_p, c_p)
        omoba = _moba_prompt(q_t, k_t, v_t, far_split, d_own_t, d_prev_t)
        x1 = _outproj(hp.reshape(-1, d_model), olin.reshape(-1, 2 * LIN_V_W), omoba.reshape(-1, MOBA_W),
                      w_o_lin, w_o_moba, g_post, tm_p)
        hp = _ffn(x1, f_pre, wg, wu, wd, f_post, tm_p, fc).reshape(nbp_, t_p, d_model)
        outs["kp"].append(jnp.transpose(k_t.reshape(nbp_, H_MOBA, HEAD_DIM, t_p), (0, 3, 1, 2)))
        outs["vp"].append(jnp.transpose(v_t.reshape(nbp_, H_MOBA, HEAD_DIM, t_p), (0, 3, 1, 2)))
        outs["rp"].append(_head_states(sfin[:, 0]))
        outs["gp"].append(_head_states(sfin[:, 1]))

        n_s = nbs * t_s
        tok, q_t, k_t, v_t = _inproj(hs, g_pre, wtok, wq_t, wk_t, wv_t, n_s)
        s0 = jnp.stack([_blockdiag_state(state_ret[l]), _blockdiag_state(state_gla[l])], axis=1)
        olin, sfin = _linrec(tok.reshape(nbs, t_s, TOK_W), cos_s, sin_s, w2p, gb, rdec, rgain, ggain, s0, t_s, c_s)
        per_batch = lambda a: jnp.transpose(a.reshape(MOBA_W, nbs, t_s), (1, 0, 2))
        q_bt, k_bt, v_bt = per_batch(q_t), per_batch(k_t), per_batch(v_t)
        sel = _sample_select(page_table, q_bt, cache_kt, l)
        sel = sel[:, :, :4 * t_s].reshape(nbs, H_MOBA, t_s, 4)[..., :MOBA_TOPK].reshape(nbs, -1)
        o_bt = _sample_attend(page_table, sel, q_bt, k_bt, v_bt, bias_rows, cache_kt, cache_vt, l)
        omoba = jnp.transpose(o_bt, (0, 2, 1)).reshape(n_s, MOBA_W).astype(BF16)
        x1 = _outproj(hs.reshape(n_s, d_model), olin.reshape(n_s, 2 * LIN_V_W), omoba, w_o_lin, w_o_moba, g_post, n_s)
        hs = _ffn(x1, f_pre, wg, wu, wd, f_post, n_s, fc).reshape(1, n_s, d_model)
        to_cache = lambda a: jnp.transpose(a.reshape(H_MOBA, HEAD_DIM, nbs, t_s), (2, 3, 0, 1))
        outs["ks"].append(to_cache(k_t))
        outs["vs"].append(to_cache(v_t))
        outs["rs"].append(_head_states(sfin[:, 0]))
        outs["gs"].append(_head_states(sfin[:, 1]))

    st = lambda k: jnp.stack(outs[k])
    return (hp, hs.reshape(nbs, t_s, d_model), st("kp"), st("vp"), st("rp"), st("gp"),
            st("ks"), st("vs"), st("rs"), st("gs"))
```

```python
import functools
import math

import jax
import jax.numpy as jnp
import numpy as np
from jax import lax
from jax.experimental import pallas as pl
from jax.experimental.pallas import tpu as pltpu

F32 = jnp.float32
BF16 = jnp.bfloat16

HEAD_DIM = 64
LIN_KEY_DIM = 32
N_LIN_HEADS = 4
H_MOBA = 8
LIN_QK_W = N_LIN_HEADS * LIN_KEY_DIM
LIN_V_W = N_LIN_HEADS * HEAD_DIM
MOBA_W = H_MOBA * HEAD_DIM
GLA_RANK = 16
GLA_TAU = 16.0
REC_CHUNK = 16
ROPE_BASE = 10000.0
MOBA_BLOCK = 256
MOBA_TOPK = 3
N_BUCKETS = 32
MAX_DISTANCE = 128
RMS_EPS = 1e-6
LIN_TOK_W = 2 * (2 * LIN_QK_W + 2 * LIN_V_W)
GA_PAD = 128
TOK_W = LIN_TOK_W + GA_PAD

VMEM_LIMIT_BYTES = 56 * 1024 * 1024
MASK_NEG = -1e30
LOG2E = 1.4426950408889634
V_ROWS = 80

_NT = (((1,), (1,)), ((), ()))
_TN = (((0,), (0,)), ((), ()))


def _cparams(*sem):
    return pltpu.CompilerParams(dimension_semantics=sem, vmem_limit_bytes=VMEM_LIMIT_BYTES)


def _rms(x, g):
    return x * lax.rsqrt(jnp.mean(x * x, axis=-1, keepdims=True) + RMS_EPS) * g


def _split3(x):
    hi = x.astype(BF16)
    r = x - hi.astype(F32)
    mid = r.astype(BF16)
    lo = (r - mid.astype(F32)).astype(BF16)
    return hi, mid, lo


def _dot(a, b):
    return jnp.dot(a, b, preferred_element_type=F32)


def _ldot3(a_bf16, x):
    hi, mid, lo = _split3(x)
    return _dot(a_bf16, hi) + _dot(a_bf16, mid) + _dot(a_bf16, lo)


def _rdot3(x, b_bf16):
    hi, mid, lo = _split3(x)
    return _dot(hi, b_bf16) + _dot(mid, b_bf16) + _dot(lo, b_bf16)


def _inproj_kernel(x_ref, g_ref, wtok_ref, wq_ref, wk_ref, wv_ref, tok_ref, q_ref, k_ref, v_ref):
    hb = _rms(x_ref[0], g_ref[...]).astype(BF16)
    tok_ref[0] = _dot(hb, wtok_ref[...])
    q_ref[0] = lax.dot_general(wq_ref[...], hb, _NT, preferred_element_type=F32)
    k_ref[0] = lax.dot_general(wk_ref[...], hb, _NT, preferred_element_type=F32)
    v_ref[0] = lax.dot_general(wv_ref[...], hb, _NT, preferred_element_type=F32)


def _inproj(x, g, wtok, wq_t, wk_t, wv_t, tm):
    nb, t, d = x.shape
    full = lambda s: pl.BlockSpec(s, lambda b, i: (0,) * len(s))
    chn = pl.BlockSpec((1, MOBA_W, tm), lambda b, i: (b, 0, i))
    return pl.pallas_call(
        _inproj_kernel,
        grid=(nb, t // tm),
        in_specs=[pl.BlockSpec((1, tm, d), lambda b, i: (b, i, 0)), full((1, d)), full((d, TOK_W)),
                  full((MOBA_W, d)), full((MOBA_W, d)), full((MOBA_W, d))],
        out_specs=[pl.BlockSpec((1, tm, TOK_W), lambda b, i: (b, i, 0)), chn, chn, chn],
        out_shape=[jax.ShapeDtypeStruct((nb, t, TOK_W), F32)] + [jax.ShapeDtypeStruct((nb, MOBA_W, t), F32)] * 3,
        compiler_params=_cparams("parallel", "parallel"),
        name="inproj",
    )(x, g, wtok, wq_t, wk_t, wv_t)


def _linrec_kernel(tok_ref, cos_ref, sin_ref, w2_ref, gb_ref, rdec_ref, rgain_ref, ggain_ref, s0_ref,
                   o_ref, sfin_ref, s_scr, *, tb, c):
    t = pl.program_id(1)

    @pl.when(t == 0)
    def _():
        s_scr[...] = s0_ref[0]

    n_sub = tb // c
    shift = int(math.log2(c))
    row = lax.broadcasted_iota(jnp.int32, (tb, tb), 0)
    col = lax.broadcasted_iota(jnp.int32, (tb, tb), 1)
    same = (row >> shift) == (col >> shift)
    tril = same & (col <= row)
    tril_b = jnp.where(tril, 1.0, 0.0).astype(BF16)
    same_b = jnp.where(same, 1.0, 0.0).astype(BF16)
    head_qk = lax.broadcasted_iota(jnp.int32, (1, LIN_QK_W), 1) // LIN_KEY_DIM
    head_v = lax.broadcasted_iota(jnp.int32, (1, LIN_V_W), 1) // HEAD_DIM
    blockdiag = (lax.broadcasted_iota(jnp.int32, (LIN_QK_W, LIN_V_W), 0) // LIN_KEY_DIM
                 == lax.broadcasted_iota(jnp.int32, (LIN_QK_W, LIN_V_W), 1) // HEAD_DIM)
    gi = lax.broadcasted_iota(jnp.int32, (LIN_V_W, LIN_V_W), 0) // HEAD_DIM
    gj = lax.broadcasted_iota(jnp.int32, (LIN_V_W, LIN_V_W), 1) // HEAD_DIM
    head_mean = jnp.where(gi == gj, 1.0 / HEAD_DIM, 0.0).astype(BF16)
    first_half = (lax.broadcasted_iota(jnp.int32, (1, LIN_QK_W), 1) % LIN_KEY_DIM) < LIN_KEY_DIM // 2
    cos = cos_ref[...]
    sin = sin_ref[...]
    qk_scale = LIN_KEY_DIM ** -0.5

    def rotary(x):
        rot = jnp.where(first_half, -pltpu.roll(x, LIN_QK_W - LIN_KEY_DIM // 2, 1), pltpu.roll(x, LIN_KEY_DIM // 2, 1))
        return x * cos + rot * sin

    def group(g_idx, q, k, v, lg, gate, gain):
        b = _ldot3(tril_b, lg)
        b_end = _ldot3(same_b, lg)
        q_dec = q * jnp.exp(b)
        k_inc = (k * jnp.exp(-b)).astype(BF16)
        k_dec = k * jnp.exp(b_end - b)
        b_end_t = b_end.T
        vb = v.astype(BF16)
        o = jnp.zeros((tb, LIN_V_W), F32)
        for h in range(N_LIN_HEADS):
            qh = jnp.where(head_qk == h, q_dec, 0.0).astype(BF16)
            sc = lax.dot_general(qh, k_inc, _NT, preferred_element_type=F32)
            p = jnp.where(tril, sc, 0.0).astype(BF16)
            o = o + jnp.where(head_v == h, _dot(p, vb), 0.0)
        s = s_scr[g_idx]
        inter = []
        for i in range(n_sub):
            lo, hi = i * c, (i + 1) * c
            inter.append(_dot(q_dec[lo:hi].astype(BF16), s.astype(BF16)))
            u = lax.dot_general(k_dec[lo:hi].astype(BF16), v[lo:hi].astype(BF16), _TN, preferred_element_type=F32)
            s = jnp.exp(b_end_t[:, lo:lo + 1]) * s + jnp.where(blockdiag, u, 0.0)
        s_scr[g_idx] = s
        o = o + (inter[0] if n_sub == 1 else jnp.concatenate(inter, axis=0))
        ms = _rdot3(o * o, head_mean)
        on = o * lax.rsqrt(ms + RMS_EPS) * gain
        return on * (gate / (1.0 + jnp.exp(-gate)))

    a = LIN_QK_W
    w = LIN_V_W
    rq = rotary(tok_ref[0, :, 0:a])
    rk = rotary(tok_ref[0, :, a:2 * a]) * qk_scale
    rv = tok_ref[0, :, 2 * a:2 * a + w]
    rgate = tok_ref[0, :, 2 * a + w:2 * a + 2 * w]
    ret_lg = jnp.broadcast_to(rdec_ref[...], (tb, LIN_QK_W))
    o_ref[0, :, 0:w] = group(0, rq, rk, rv, ret_lg, rgate, rgain_ref[...]).astype(o_ref.dtype)

    base = 2 * a + 2 * w
    gq = tok_ref[0, :, base:base + a] * qk_scale
    gk = tok_ref[0, :, base + a:base + 2 * a]
    gv = tok_ref[0, :, base + 2 * a:base + 2 * a + w]
    ggate = tok_ref[0, :, base + 2 * a + w:base + 2 * a + 2 * w]
    ga = tok_ref[0, :, LIN_TOK_W:TOK_W]
    z = _dot(ga.astype(BF16), w2_ref[...]) + gb_ref[...]
    gla_lg = (jnp.minimum(z, 0.0) - jnp.log(1.0 + jnp.exp(-jnp.abs(z)))) * (1.0 / GLA_TAU)
    o_ref[0, :, w:2 * w] = group(1, gq, gk, gv, gla_lg, ggate, ggain_ref[...]).astype(o_ref.dtype)

    @pl.when(t == pl.num_programs(1) - 1)
    def _():
        sfin_ref[0] = s_scr[...]


def _linrec(tok, cos, sin, w2p, gb, rdec, rgain, ggain, s0, tb, c):
    nb, t, _ = tok.shape
    full = lambda s: pl.BlockSpec(s, lambda b, i: (0,) * len(s))
    st = pl.BlockSpec((1, 2, LIN_QK_W, LIN_V_W), lambda b, i: (b, 0, 0, 0))
    return pl.pallas_call(
        functools.partial(_linrec_kernel, tb=tb, c=c),
        grid=(nb, t // tb),
        in_specs=[pl.BlockSpec((1, tb, TOK_W), lambda b, i: (b, i, 0)),
                  pl.BlockSpec((tb, LIN_QK_W), lambda b, i: (i, 0)),
                  pl.BlockSpec((tb, LIN_QK_W), lambda b, i: (i, 0)),
                  full((GA_PAD, LIN_QK_W)), full((1, LIN_QK_W)), full((1, LIN_QK_W)),
                  full((1, LIN_V_W)), full((1, LIN_V_W)), st],
        out_specs=[pl.BlockSpec((1, tb, 2 * LIN_V_W), lambda b, i: (b, i, 0)), st],
        out_shape=[jax.ShapeDtypeStruct((nb, t, 2 * LIN_V_W), BF16),
                   jax.ShapeDtypeStruct((nb, 2, LIN_QK_W, LIN_V_W), F32)],
        scratch_shapes=[pltpu.VMEM((2, LIN_QK_W, LIN_V_W), F32)],
        compiler_params=_cparams("parallel", "arbitrary"),
        name="linrec",
    )(tok, cos, sin, w2p, gb, rdec, rgain, ggain, s0)


def _select_topk(gate_t, n_valid, nbp, width):
    rown = lax.broadcasted_iota(jnp.int32, (nbp, width), 0)
    rown_f = rown.astype(F32)
    valid = rown < n_valid
    g = jnp.where(valid, gate_t, -jnp.inf)
    sel = jnp.zeros((nbp, width), F32)
    for _ in range(MOBA_TOPK):
        mx = jnp.max(g, axis=0, keepdims=True)
        idx = jnp.min(jnp.where(g == mx, rown_f, float(nbp)), axis=0, keepdims=True)
        pick = rown_f == idx
        sel = jnp.where(pick, jnp.where(valid, 1.0, 0.0), sel)
        g = jnp.where(pick, -jnp.inf, g)
    return sel, rown


def _moba_prompt_kernel(bias_ref, q_ref, k_ref, v_ref, bucket_ref, o_ref,
                        kaug_scr, vaug_scr, kmean_scr, near_scr, s_scr, p_scr, *, nb, nbp):
    hp = pl.program_id(1)
    qi = pl.program_id(2)
    blk = MOBA_BLOCK
    n_tail = HEAD_DIM - nbp
    c_far, c_dummy = nbp, nbp + 2

    @pl.when(qi == 0)
    def _():
        lane = lax.broadcasted_iota(jnp.int32, (blk, HEAD_DIM), 1)
        lane_aug = lax.broadcasted_iota(jnp.int32, (blk, 2 * HEAD_DIM), 1)
        ones_row = jnp.where(lax.broadcasted_iota(jnp.int32, (V_ROWS - HEAD_DIM, blk), 0) == 0, 1.0, 0.0).astype(BF16)
        for hh in range(2):
            head = hp * 2 + hh
            rows = slice(hh * HEAD_DIM, (hh + 1) * HEAD_DIM)
            kaug_scr[hh, 0:blk, :] = jnp.where(lane_aug == HEAD_DIM + c_dummy, 1.0, 0.0).astype(BF16)
            vaug_scr[hh, 0:HEAD_DIM, 0:blk] = jnp.zeros((HEAD_DIM, blk), BF16)
            vaug_scr[hh, HEAD_DIM:V_ROWS, 0:blk] = ones_row
            kmean_scr[hh] = jnp.zeros((nbp, HEAD_DIM), F32)

            def fill(n, carry):
                off = pl.multiple_of(n * blk, blk)
                dst = pl.ds(pl.multiple_of((n + 1) * blk, blk), blk)
                kb = k_ref[0, rows, pl.ds(off, blk)].T
                kmean_scr[hh, pl.ds(n, 1), :] = jnp.mean(kb, axis=0, keepdims=True)
                extra = jnp.where((lane == n) | (lane == c_far) | (lane == c_far + 1), 1.0, 0.0)
                kaug_scr[hh, dst, :] = jnp.concatenate([kb, extra], axis=1).astype(BF16)
                vaug_scr[hh, 0:HEAD_DIM, dst] = v_ref[0, rows, pl.ds(off, blk)].astype(BF16)
                vaug_scr[hh, HEAD_DIM:V_ROWS, dst] = ones_row
                return carry

            lax.fori_loop(0, nb, fill, 0)

            far_b = bias_ref[head * N_BUCKETS + N_BUCKETS - 1]
            for c0 in range(0, 2 * blk, 128):
                bk = bucket_ref[c0:c0 + 128, :]

                def pick(bi, tab):
                    return jnp.where(bk == bi, (bias_ref[head * N_BUCKETS + bi] - far_b) * LOG2E, tab)

                near_scr[hh, c0:c0 + 128, :] = lax.fori_loop(0, N_BUCKETS, pick, jnp.where(bk < 0, MASK_NEG, 0.0))

    tail_row = lax.broadcasted_iota(jnp.int32, (n_tail, blk), 0)
    near_off = pl.multiple_of(qi * blk, blk)
    state = []
    q_far = []
    for hh in range(2):
        head = hp * 2 + hh
        q_t = q_ref[0, hh * HEAD_DIM:(hh + 1) * HEAD_DIM, :]
        gate_t = jnp.dot(kmean_scr[hh], q_t, preferred_element_type=F32, precision=lax.Precision.HIGHEST)
        sel, rown = _select_topk(gate_t, qi, nbp, blk)
        near_ok = (rown == qi) | ((rown == qi - 1) & (sel > 0.0))
        far_ok = (rown < qi - 1) & (sel > 0.0)
        far_v = jnp.full((n_tail, blk), bias_ref[head * N_BUCKETS + N_BUCKETS - 1] * LOG2E, F32)
        far_hi = far_v.astype(BF16).astype(F32)
        tail = jnp.where(tail_row == 0, far_hi, jnp.where(tail_row == 1, far_v - far_hi,
                                                          jnp.where(tail_row == 2, MASK_NEG, 0.0)))
        qs = q_t * (HEAD_DIM ** -0.5 * LOG2E)
        q_near = jnp.concatenate([qs, jnp.where(near_ok, 0.0, MASK_NEG), tail], axis=0).astype(BF16)
        q_far.append(jnp.concatenate([qs, jnp.where(far_ok, 0.0, MASK_NEG), tail], axis=0).astype(BF16))
        s_scr[hh] = _dot(kaug_scr[hh, pl.ds(near_off, 2 * blk), :], q_near) + near_scr[hh]
    for hh in range(2):
        m = jnp.max(s_scr[hh], axis=0, keepdims=True)
        p_scr[hh] = jnp.exp2(s_scr[hh] - m).astype(BF16)
        state += [m, _dot(vaug_scr[hh, :, pl.ds(near_off, 2 * blk)], p_scr[hh])]

    def far_tiles(k, carry):
        off = pl.multiple_of(k * 2 * blk, 2 * blk)
        out = []
        for hh in range(2):
            s_scr[hh] = _dot(kaug_scr[hh, pl.ds(off, 2 * blk), :], q_far[hh])
        for hh in range(2):
            m, acc = carry[2 * hh], carry[2 * hh + 1]
            m_new = jnp.maximum(m, jnp.max(s_scr[hh], axis=0, keepdims=True))
            p_scr[hh] = jnp.exp2(s_scr[hh] - m_new).astype(BF16)
            acc = acc * jnp.exp2(m - m_new) + _dot(vaug_scr[hh, :, pl.ds(off, 2 * blk)], p_scr[hh])
            out += [m_new, acc]
        return tuple(out)

    state = lax.fori_loop(0, (qi + 1) // 2, far_tiles, tuple(state))
    for hh in range(2):
        acc = state[2 * hh + 1]
        o = acc[0:HEAD_DIM] / acc[HEAD_DIM:HEAD_DIM + 1]
        o_ref[0, :, hh * HEAD_DIM:(hh + 1) * HEAD_DIM] = o.T.astype(o_ref.dtype)


def _moba_prompt(q_t, k_t, v_t, bias_flat, bucket_near):
    nbat, _, t = q_t.shape
    nb = t // MOBA_BLOCK
    nbp = max(8, -(-nb // 8) * 8)
    assert t % MOBA_BLOCK == 0 and nbp + 3 <= HEAD_DIM
    kv = pl.BlockSpec((1, 2 * HEAD_DIM, t), lambda b, hp, qi, bias: (b, hp, 0))
    return pl.pallas_call(
        functools.partial(_moba_prompt_kernel, nb=nb, nbp=nbp),
        grid_spec=pltpu.PrefetchScalarGridSpec(
            num_scalar_prefetch=1,
            grid=(nbat, H_MOBA // 2, nb),
            in_specs=[pl.BlockSpec((1, 2 * HEAD_DIM, MOBA_BLOCK), lambda b, hp, qi, bias: (b, hp, qi)), kv, kv,
                      pl.BlockSpec((2 * MOBA_BLOCK, MOBA_BLOCK), lambda b, hp, qi, bias: (0, 0))],
            out_specs=pl.BlockSpec((1, MOBA_BLOCK, 2 * HEAD_DIM), lambda b, hp, qi, bias: (b, qi, hp)),
            scratch_shapes=[pltpu.VMEM((2, t + MOBA_BLOCK, 2 * HEAD_DIM), BF16),
                            pltpu.VMEM((2, V_ROWS, t + MOBA_BLOCK), BF16),
                            pltpu.VMEM((2, nbp, HEAD_DIM), F32),
                            pltpu.VMEM((2, 2 * MOBA_BLOCK, MOBA_BLOCK), F32),
                            pltpu.VMEM((2, 2 * MOBA_BLOCK, MOBA_BLOCK), F32),
                            pltpu.VMEM((2, 2 * MOBA_BLOCK, MOBA_BLOCK), BF16)],
        ),
        out_shape=jax.ShapeDtypeStruct((nbat, t, MOBA_W), BF16),
        compiler_params=_cparams("parallel", "parallel", "arbitrary"),
        name="moba_prompt",
    )(bias_flat, q_t, k_t, v_t, bucket_near)


N_PAGE_BUF = 8


def _sample_select_kernel(pt_ref, q_ref, cache_ref, sel_ref, buf, sem, km_scr, *, layer, n_pages, pages_per_block):
    b = pl.program_id(0)
    nblk = n_pages // pages_per_block
    page = cache_ref.shape[-1]

    def copy(p, slot):
        return pltpu.make_async_copy(cache_ref.at[layer, pt_ref[b, p]], buf.at[slot], sem.at[slot])

    for p in range(N_PAGE_BUF):
        copy(p, p).start()
    km_scr[...] = jnp.zeros(km_scr.shape, F32)

    def block(j, carry):
        tot = None
        for u in range(pages_per_block):
            p = j * pages_per_block + u
            slot = p % N_PAGE_BUF
            copy(p, slot).wait()
            x = buf[slot].reshape(MOBA_W, page)
            tot = x if tot is None else tot + x

            @pl.when(p + N_PAGE_BUF < n_pages)
            def _():
                copy(p + N_PAGE_BUF, slot).start()

        km_scr[pl.ds(j, 1), :] = jnp.sum(tot.T, axis=0, keepdims=True) * (1.0 / (pages_per_block * page))
        return carry

    lax.fori_loop(0, nblk, block, 0)

    km = km_scr[...]
    nrow = km.shape[0]
    head_sum = (lax.broadcasted_iota(jnp.int32, (MOBA_W, 128), 0) // HEAD_DIM
                == lax.broadcasted_iota(jnp.int32, (MOBA_W, 128), 1))
    head_sum = jnp.where(head_sum, 1.0, 0.0).astype(BF16)
    rown = lax.broadcasted_iota(jnp.int32, (nrow, 128), 0)
    rown_f = rown.astype(F32)
    sel_ref[0] = jnp.zeros(sel_ref.shape[1:], jnp.int32)
    for t in range(q_ref.shape[1]):
        gate = _rdot3(km * q_ref[0, t:t + 1, :], head_sum)
        g = jnp.where(rown < nblk, gate, -jnp.inf)
        for r in range(MOBA_TOPK):
            mx = jnp.max(g, axis=0, keepdims=True)
            idx = jnp.min(jnp.where(g == mx, rown_f, float(nrow)), axis=0, keepdims=True)
            sel_ref[0, 4 * t + r:4 * t + r + 1, :] = idx.astype(jnp.int32)
            g = jnp.where(rown_f == idx, -jnp.inf, g)


def _sample_select(page_table, q_tb, cache_t, layer):
    nbat, n_pages = page_table.shape
    n_t = q_tb.shape[1]
    page = cache_t.shape[-1]
    ppb = MOBA_BLOCK // page
    nblk = n_pages // ppb
    assert n_pages % ppb == 0 and n_pages >= N_PAGE_BUF
    sel_rows = -(-4 * n_t // 8) * 8
    return pl.pallas_call(
        functools.partial(_sample_select_kernel, layer=layer, n_pages=n_pages, pages_per_block=ppb),
        grid_spec=pltpu.PrefetchScalarGridSpec(
            num_scalar_prefetch=1,
            grid=(nbat,),
            in_specs=[pl.BlockSpec((1, n_t, MOBA_W), lambda b, pt: (b, 0, 0)),
                      pl.BlockSpec(memory_space=pl.ANY)],
            out_specs=pl.BlockSpec((1, sel_rows, 128), lambda b, pt: (b, 0, 0)),
            scratch_shapes=[pltpu.VMEM((N_PAGE_BUF, H_MOBA, HEAD_DIM, page), F32),
                            pltpu.SemaphoreType.DMA((N_PAGE_BUF,)),
                            pltpu.VMEM((-(-nblk // 8) * 8, MOBA_W), F32)],
        ),
        out_shape=jax.ShapeDtypeStruct((nbat, sel_rows, 128), jnp.int32),
        compiler_params=_cparams("arbitrary"),
        name="sample_select",
    )(page_table, q_tb, cache_t)


def _sample_attend_kernel(pt_ref, sel_ref, q_ref, kn_ref, vn_ref, bias_ref, ck_ref, cv_ref, o_ref,
                          kbuf, vbuf, sem, *, layer, n_t, pages_per_block, nblk):
    b = pl.program_id(0)
    page = ck_ref.shape[-1]
    n_slot = n_t * MOBA_TOPK

    def copies(h, par):
        out = []
        for t in range(n_t):
            for r in range(MOBA_TOPK):
                j = sel_ref[b, (h * n_t + t) * MOBA_TOPK + r]
                slot = t * MOBA_TOPK + r
                for u in range(pages_per_block):
                    pid = pt_ref[b, j * pages_per_block + u]
                    lanes = pl.ds(u * page, page)
                    out.append(pltpu.make_async_copy(ck_ref.at[layer, pid, h], kbuf.at[par, slot, :, lanes], sem.at[par, 0]))
                    out.append(pltpu.make_async_copy(cv_ref.at[layer, pid, h], vbuf.at[par, slot, :, lanes], sem.at[par, 1]))
        return out

    for cp in copies(0, 0):
        cp.start()
    key_off = lax.broadcasted_iota(jnp.int32, (1, n_t), 1)

    def per_head(h, carry):
        par = h % 2

        @pl.when(h + 1 < H_MOBA)
        def _():
            for cp in copies(h + 1, 1 - par):
                cp.start()

        for cp in copies(h, par):
            cp.wait()
        rows = pl.ds(pl.multiple_of(h * HEAD_DIM, HEAD_DIM), HEAD_DIM)
        qh = q_ref[0, rows, :] * (HEAD_DIM ** -0.5)
        kn = kn_ref[0, rows, :]
        vn = vn_ref[0, rows, :]
        for t in range(n_t):
            qcol = qh[:, t:t + 1]
            s_own = jnp.sum(qcol * kn, axis=0, keepdims=True) + bias_ref[h, n_t + 1 + t:n_t + 2 + t, 0:n_t]
            s_own = jnp.where(key_off <= t, s_own, MASK_NEG)
            ss = []
            for r in range(MOBA_TOPK):
                j = sel_ref[b, (h * n_t + t) * MOBA_TOPK + r]
                s = jnp.sum(qcol * kbuf[par, t * MOBA_TOPK + r], axis=0, keepdims=True)
                ss.append(s + jnp.where(j == nblk - 1, bias_ref[h, t:t + 1, :], bias_ref[h, n_t:n_t + 1, :]))
            m = jnp.max(s_own, axis=1, keepdims=True)
            for s in ss:
                m = jnp.maximum(m, jnp.max(s, axis=1, keepdims=True))
            p_own = jnp.exp(s_own - m)
            l = jnp.sum(p_own, axis=1, keepdims=True)
            acc = jnp.sum(p_own * vn, axis=1, keepdims=True)
            for r, s in enumerate(ss):
                p = jnp.exp(s - m)
                l = l + jnp.sum(p, axis=1, keepdims=True)
                acc = acc + jnp.sum(p * vbuf[par, t * MOBA_TOPK + r], axis=1, keepdims=True)
            o_ref[0, rows, t:t + 1] = acc / l
        return carry

    lax.fori_loop(0, H_MOBA, per_head, 0)


def _sample_attend(page_table, sel, q_bt, kn_bt, vn_bt, bias_rows, cache_kt, cache_vt, layer):
    nbat, n_pages = page_table.shape
    n_t = q_bt.shape[-1]
    page = cache_kt.shape[-1]
    ppb = MOBA_BLOCK // page
    n_slot = n_t * MOBA_TOPK
    new = pl.BlockSpec((1, MOBA_W, n_t), lambda b, pt, sl: (b, 0, 0))
    return pl.pallas_call(
        functools.partial(_sample_attend_kernel, layer=layer, n_t=n_t, pages_per_block=ppb, nblk=n_pages // ppb),
        grid_spec=pltpu.PrefetchScalarGridSpec(
            num_scalar_prefetch=2,
            grid=(nbat,),
            in_specs=[new, new, new,
                      pl.BlockSpec(bias_rows.shape, lambda b, pt, sl: (0, 0, 0)),
                      pl.BlockSpec(memory_space=pl.ANY), pl.BlockSpec(memory_space=pl.ANY)],
            out_specs=new,
            scratch_shapes=[pltpu.VMEM((2, n_slot, HEAD_DIM, MOBA_BLOCK), F32),
                            pltpu.VMEM((2, n_slot, HEAD_DIM, MOBA_BLOCK), F32),
                            pltpu.SemaphoreType.DMA((2, 2))],
        ),
        out_shape=jax.ShapeDtypeStruct((nbat, MOBA_W, n_t), F32),
        compiler_params=_cparams("arbitrary"),
        name="sample_attend",
    )(page_table, sel, q_bt, kn_bt, vn_bt, bias_rows, cache_kt, cache_vt)


def _outproj_kernel(x_ref, olin_ref, omoba_ref, wl_ref, wm_ref, g_ref, y_ref):
    mix = _dot(olin_ref[...], wl_ref[...]) + _dot(omoba_ref[...], wm_ref[...])
    y_ref[...] = x_ref[...] + _rms(mix, g_ref[...])


def _outproj(x, olin, omoba, w_lin, w_moba, g, tm):
    n, d = x.shape
    full = lambda s: pl.BlockSpec(s, lambda i: (0,) * len(s))
    rows = lambda w: pl.BlockSpec((tm, w), lambda i: (i, 0))
    return pl.pallas_call(
        _outproj_kernel,
        grid=(n // tm,),
        in_specs=[rows(d), rows(olin.shape[1]), rows(omoba.shape[1]), full(w_lin.shape), full(w_moba.shape), full((1, d))],
        out_specs=rows(d),
        out_shape=jax.ShapeDtypeStruct((n, d), F32),
        compiler_params=_cparams("parallel"),
        name="outproj",
    )(x, olin, omoba, w_lin, w_moba, g)


def _ffn_kernel(x_ref, gpre_ref, wg_ref, wu_ref, wd_ref, gpost_ref, y_ref, h_scr, acc_scr):
    j = pl.program_id(1)

    @pl.when(j == 0)
    def _():
        h_scr[...] = _rms(x_ref[...], gpre_ref[...]).astype(BF16)
        acc_scr[...] = jnp.zeros(acc_scr.shape, F32)

    h = h_scr[...]
    gate = _dot(h, wg_ref[...])
    up = _dot(h, wu_ref[...])
    act = (gate / (1.0 + jnp.exp(-gate)) * up).astype(BF16)
    acc_scr[...] += _dot(act, wd_ref[...])

    @pl.when(j == pl.num_programs(1) - 1)
    def _():
        y_ref[...] = x_ref[...] + _rms(acc_scr[...], gpost_ref[...])


def _ffn(x, gpre, wg, wu, wd, gpost, tm, fc):
    n, d = x.shape
    dff = wg.shape[1]
    rows = pl.BlockSpec((tm, d), lambda i, j: (i, 0))
    vec = pl.BlockSpec((1, d), lambda i, j: (0, 0))
    return pl.pallas_call(
        _ffn_kernel,
        grid=(n // tm, dff // fc),
        in_specs=[rows, vec, pl.BlockSpec((d, fc), lambda i, j: (0, j)), pl.BlockSpec((d, fc), lambda i, j: (0, j)),
                  pl.BlockSpec((fc, d), lambda i, j: (j, 0)), vec],
        out_specs=rows,
        out_shape=jax.ShapeDtypeStruct((n, d), F32),
        scratch_shapes=[pltpu.VMEM((tm, d), BF16), pltpu.VMEM((tm, d), F32)],
        compiler_params=_cparams("parallel", "arbitrary"),
        name="ffn",
    )(x, gpre, wg, wu, wd, gpost)


def _t5_bucket_np(dist):
    n = np.maximum(dist, 0)
    max_exact = N_BUCKETS // 2
    nf = np.maximum(n, 1).astype(np.float32)
    large = max_exact + (np.log(nf / np.float32(max_exact)) / np.float32(math.log(MAX_DISTANCE / max_exact))
                         * np.float32(N_BUCKETS - max_exact)).astype(np.int32)
    large = np.minimum(large, N_BUCKETS - 1)
    return np.where(n < max_exact, n, large).astype(np.int32)


def _rotary_tables(pos):
    half = LIN_KEY_DIM // 2
    inv = 1.0 / (ROPE_BASE ** jnp.linspace(0.0, 1.0, half, dtype=F32))
    ang = pos.astype(F32)[:, None] * inv[None, :]
    tile = lambda a: jnp.tile(a, (1, 2 * N_LIN_HEADS))
    return tile(jnp.cos(ang)), tile(jnp.sin(ang))


def _blockdiag_state(s):
    eye = jnp.eye(N_LIN_HEADS, dtype=s.dtype)
    return (s[:, :, :, None, :] * eye[None, :, None, :, None]).reshape(s.shape[0], LIN_QK_W, LIN_V_W)


def _head_states(sb):
    x = sb.reshape(sb.shape[0], N_LIN_HEADS, LIN_KEY_DIM, N_LIN_HEADS, HEAD_DIM)
    return jnp.stack([x[:, h, :, h, :] for h in range(N_LIN_HEADS)], axis=1)


def _row_tile(n, want):
    t = min(n, want)
    while n % t:
        t //= 2
    return t


def kernel(x_prompt, x_sample, cache_k, cache_v, state_ret, state_gla, page_table, w_in, w_out, gla_gate_w2, gla_gate_b, ret_norm_g, gla_norm_g, norm_attn_pre, norm_attn_post, norm_ffn_pre, norm_ffn_post, w_ffn_gate, w_ffn_up, w_ffn_down, rel_bias):
    depth = w_in.shape[0]
    nbp_, t_p, d_model = x_prompt.shape
    nbs, t_s, _ = x_sample.shape
    n_pages = page_table.shape[1]
    page = cache_k.shape[2]
    past_len = n_pages * page
    assert past_len % MOBA_BLOCK == 0 and t_s <= MOBA_BLOCK and t_p % MOBA_BLOCK == 0
    dff = w_ffn_gate.shape[2]

    cos_p, sin_p = _rotary_tables(jnp.arange(t_p, dtype=jnp.int32))
    cos_s, sin_s = _rotary_tables(past_len + jnp.arange(t_s, dtype=jnp.int32))
    off = np.arange(MOBA_BLOCK)
    dist_own = off[None, :] - off[:, None]
    bucket_near = np.concatenate([_t5_bucket_np(MOBA_BLOCK + dist_own),
                                  np.where(dist_own >= 0, _t5_bucket_np(dist_own), -1)], axis=0).astype(np.int32)
    bias_h = rel_bias.astype(F32).T
    bias_flat = bias_h.reshape(-1)
    far = bias_h[:, N_BUCKETS - 1]
    prev_rows = bias_h[:, _t5_bucket_np(MOBA_BLOCK + np.arange(t_s)[:, None] - off[None, :])]
    far_row = jnp.broadcast_to(far[:, None, None], (H_MOBA, 1, MOBA_BLOCK))
    own_rows = bias_h[:, _t5_bucket_np(np.arange(t_s)[:, None] - off[None, :])]
    n_rows = 2 * t_s + 1
    pad_rows = -(-n_rows // 8) * 8 - n_rows
    bias_rows = jnp.concatenate([prev_rows, far_row, own_rows, jnp.zeros((H_MOBA, pad_rows, MOBA_BLOCK), F32)], axis=1)

    rdec = jnp.repeat(jnp.log(1.0 - 2.0 ** (-5.0 - jnp.arange(N_LIN_HEADS, dtype=F32))), LIN_KEY_DIM)[None, :]
    cache_kt = jnp.transpose(cache_k, (0, 1, 3, 4, 2))
    cache_vt = jnp.transpose(cache_v, (0, 1, 3, 4, 2))
    w_in_t = jnp.swapaxes(w_in, 1, 2)

    hp = x_prompt
    hs = x_sample.reshape(1, nbs * t_s, d_model)
    zero_state = jnp.zeros((nbp_, 2, LIN_QK_W, LIN_V_W), F32)
    tm_p = _row_tile(t_p, 512)
    tb_p = _row_tile(t_p, 128)
    c_p = math.gcd(t_p, REC_CHUNK)
    c_s = math.gcd(t_s, REC_CHUNK)
    fc = dff // 2 if (dff // 2) % 128 == 0 else dff
    outs = {k: [] for k in ("kp", "vp", "rp", "gp", "ks", "vs", "rs", "gs")}

    for l in range(depth):
        seg = w_in_t[l]
        wtok = jnp.concatenate([seg[:LIN_TOK_W + GLA_RANK], jnp.zeros((GA_PAD - GLA_RANK, d_model), F32)], axis=0).T.astype(BF16)
        m0 = LIN_TOK_W + GLA_RANK
        wq_t = seg[m0:m0 + MOBA_W].astype(BF16)
        wk_t = seg[m0 + MOBA_W:m0 + 2 * MOBA_W].astype(BF16)
        wv_t = seg[m0 + 2 * MOBA_W:m0 + 3 * MOBA_W].astype(BF16)
        w2p = jnp.concatenate([gla_gate_w2[l], jnp.zeros((GA_PAD - GLA_RANK, LIN_QK_W), F32)], axis=0).astype(BF16)
        gb = gla_gate_b[l][None, :]
        rgain = jnp.tile(ret_norm_g[l], N_LIN_HEADS)[None, :]
        ggain = jnp.tile(gla_norm_g[l], N_LIN_HEADS)[None, :]
        w_o = w_out[l].astype(BF16)
        w_o_lin, w_o_moba = w_o[:2 * LIN_V_W], w_o[2 * LIN_V_W:]
        wg, wu, wd = w_ffn_gate[l].astype(BF16), w_ffn_up[l].astype(BF16), w_ffn_down[l].astype(BF16)
        g_pre, g_post = norm_attn_pre[l][None, :], norm_attn_post[l][None, :]
        f_pre, f_post = norm_ffn_pre[l][None, :], norm_ffn_post[l][None, :]

        tok, q_t, k_t, v_t = _inproj(hp, g_pre, wtok, wq_t, wk_t, wv_t, tm_p)
        olin, sfin = _linrec(tok, cos_p, sin_p, w2p, gb, rdec, rgain, ggain, zero_state, tb_p, c_p)
        omoba = _moba_prompt(q_t, k_t, v_t, bias_flat, bucket_near)
        x1 = _outproj(hp.reshape(-1, d_model), olin.reshape(-1, 2 * LIN_V_W), omoba.reshape(-1, MOBA_W),
                      w_o_lin, w_o_moba, g_post, tm_p)
        hp = _ffn(x1, f_pre, wg, wu, wd, f_post, tm_p, fc).reshape(nbp_, t_p, d_model)
        outs["kp"].append(jnp.transpose(k_t.reshape(nbp_, H_MOBA, HEAD_DIM, t_p), (0, 3, 1, 2)))
        outs["vp"].append(jnp.transpose(v_t.reshape(nbp_, H_MOBA, HEAD_DIM, t_p), (0, 3, 1, 2)))
        outs["rp"].append(_head_states(sfin[:, 0]))
        outs["gp"].append(_head_states(sfin[:, 1]))

        n_s = nbs * t_s
        tok, q_t, k_t, v_t = _inproj(hs, g_pre, wtok, wq_t, wk_t, wv_t, n_s)
        s0 = jnp.stack([_blockdiag_state(state_ret[l]), _blockdiag_state(state_gla[l])], axis=1)
        olin, sfin = _linrec(tok.reshape(nbs, t_s, TOK_W), cos_s, sin_s, w2p, gb, rdec, rgain, ggain, s0, t_s, c_s)
        per_batch = lambda a: jnp.transpose(a.reshape(MOBA_W, nbs, t_s), (1, 0, 2))
        q_bt, k_bt, v_bt = per_batch(q_t), per_batch(k_t), per_batch(v_t)
        sel = _sample_select(page_table, jnp.transpose(q_bt, (0, 2, 1)), cache_kt, l)
        sel = jnp.transpose(sel[:, :4 * t_s, :H_MOBA].reshape(nbs, t_s, 4, H_MOBA)[:, :, :MOBA_TOPK], (0, 3, 1, 2))
        sel = sel.reshape(nbs, -1)
        o_bt = _sample_attend(page_table, sel, q_bt, k_bt, v_bt, bias_rows, cache_kt, cache_vt, l)
        omoba = jnp.transpose(o_bt, (0, 2, 1)).reshape(n_s, MOBA_W).astype(BF16)
        x1 = _outproj(hs.reshape(n_s, d_model), olin.reshape(n_s, 2 * LIN_V_W), omoba, w_o_lin, w_o_moba, g_post, n_s)
        hs = _ffn(x1, f_pre, wg, wu, wd, f_post, n_s, fc).reshape(1, n_s, d_model)
        to_cache = lambda a: jnp.transpose(a.reshape(H_MOBA, HEAD_DIM, nbs, t_s), (2, 3, 0, 1))
        outs["ks"].append(to_cache(k_t))
        outs["vs"].append(to_cache(v_t))
        outs["rs"].append(_head_states(sfin[:, 0]))
        outs["gs"].append(_head_states(sfin[:, 1]))

    st = lambda k: jnp.stack(outs[k])
    return (hp, hs.reshape(nbs, t_s, d_model), st("kp"), st("vp"), st("rp"), st("gp"),
            st("ks"), st("vs"), st("rs"), st("gs"))
```

```python
import functools
import math

import jax
import jax.numpy as jnp
import numpy as np
from jax import lax
from jax.experimental import pallas as pl
from jax.experimental.pallas import tpu as pltpu

F32 = jnp.float32
BF16 = jnp.bfloat16

HEAD_DIM = 64
LIN_KEY_DIM = 32
N_LIN_HEADS = 4
H_MOBA = 8
LIN_QK_W = N_LIN_HEADS * LIN_KEY_DIM
LIN_V_W = N_LIN_HEADS * HEAD_DIM
MOBA_W = H_MOBA * HEAD_DIM
GLA_RANK = 16
GLA_TAU = 16.0
REC_CHUNK = 16
ROPE_BASE = 10000.0
MOBA_BLOCK = 256
MOBA_TOPK = 3
N_BUCKETS = 32
MAX_DISTANCE = 128
RMS_EPS = 1e-6
LIN_TOK_W = 2 * (2 * LIN_QK_W + 2 * LIN_V_W)
GA_PAD = 128
TOK_W = LIN_TOK_W + GA_PAD

VMEM_LIMIT_BYTES = 56 * 1024 * 1024
MASK_NEG = -1e30
LOG2E = 1.4426950408889634
V_ROWS = 80

_NT = (((1,), (1,)), ((), ()))
_TN = (((0,), (0,)), ((), ()))


def _cparams(*sem):
    return pltpu.CompilerParams(dimension_semantics=sem, vmem_limit_bytes=VMEM_LIMIT_BYTES)


def _rms(x, g):
    return x * lax.rsqrt(jnp.mean(x * x, axis=-1, keepdims=True) + RMS_EPS) * g


def _split3(x):
    hi = x.astype(BF16)
    r = x - hi.astype(F32)
    mid = r.astype(BF16)
    lo = (r - mid.astype(F32)).astype(BF16)
    return hi, mid, lo


def _dot(a, b):
    return jnp.dot(a, b, preferred_element_type=F32)


def _ldot3(a_bf16, x):
    hi, mid, lo = _split3(x)
    return _dot(a_bf16, hi) + _dot(a_bf16, mid) + _dot(a_bf16, lo)


def _rdot3(x, b_bf16):
    hi, mid, lo = _split3(x)
    return _dot(hi, b_bf16) + _dot(mid, b_bf16) + _dot(lo, b_bf16)


def _inproj_kernel(x_ref, g_ref, wtok_ref, wq_ref, wk_ref, wv_ref, tok_ref, q_ref, k_ref, v_ref):
    hb = _rms(x_ref[0], g_ref[...]).astype(BF16)
    tok_ref[0] = _dot(hb, wtok_ref[...])
    q_ref[0] = lax.dot_general(wq_ref[...], hb, _NT, preferred_element_type=F32)
    k_ref[0] = lax.dot_general(wk_ref[...], hb, _NT, preferred_element_type=F32)
    v_ref[0] = lax.dot_general(wv_ref[...], hb, _NT, preferred_element_type=F32)


def _inproj(x, g, wtok, wq_t, wk_t, wv_t, tm):
    nb, t, d = x.shape
    full = lambda s: pl.BlockSpec(s, lambda b, i: (0,) * len(s))
    chn = pl.BlockSpec((1, MOBA_W, tm), lambda b, i: (b, 0, i))
    return pl.pallas_call(
        _inproj_kernel,
        grid=(nb, t // tm),
        in_specs=[pl.BlockSpec((1, tm, d), lambda b, i: (b, i, 0)), full((1, d)), full((d, TOK_W)),
                  full((MOBA_W, d)), full((MOBA_W, d)), full((MOBA_W, d))],
        out_specs=[pl.BlockSpec((1, tm, TOK_W), lambda b, i: (b, i, 0)), chn, chn, chn],
        out_shape=[jax.ShapeDtypeStruct((nb, t, TOK_W), F32)] + [jax.ShapeDtypeStruct((nb, MOBA_W, t), F32)] * 3,
        compiler_params=_cparams("parallel", "parallel"),
        name="inproj",
    )(x, g, wtok, wq_t, wk_t, wv_t)


def _linrec_kernel(tok_ref, cos_ref, sin_ref, w2_ref, gb_ref, rdec_ref, rgain_ref, ggain_ref, s0_ref,
                   o_ref, sfin_ref, s_scr, *, tb, c):
    t = pl.program_id(1)

    @pl.when(t == 0)
    def _():
        s_scr[...] = s0_ref[0]

    n_sub = tb // c
    shift = int(math.log2(c))
    row = lax.broadcasted_iota(jnp.int32, (tb, tb), 0)
    col = lax.broadcasted_iota(jnp.int32, (tb, tb), 1)
    same = (row >> shift) == (col >> shift)
    tril = same & (col <= row)
    tril_b = jnp.where(tril, 1.0, 0.0).astype(BF16)
    same_b = jnp.where(same, 1.0, 0.0).astype(BF16)
    head_qk = lax.broadcasted_iota(jnp.int32, (1, LIN_QK_W), 1) // LIN_KEY_DIM
    head_v = lax.broadcasted_iota(jnp.int32, (1, LIN_V_W), 1) // HEAD_DIM
    blockdiag = (lax.broadcasted_iota(jnp.int32, (LIN_QK_W, LIN_V_W), 0) // LIN_KEY_DIM
                 == lax.broadcasted_iota(jnp.int32, (LIN_QK_W, LIN_V_W), 1) // HEAD_DIM)
    gi = lax.broadcasted_iota(jnp.int32, (LIN_V_W, LIN_V_W), 0) // HEAD_DIM
    gj = lax.broadcasted_iota(jnp.int32, (LIN_V_W, LIN_V_W), 1) // HEAD_DIM
    head_mean = jnp.where(gi == gj, 1.0 / HEAD_DIM, 0.0).astype(BF16)
    first_half = (lax.broadcasted_iota(jnp.int32, (1, LIN_QK_W), 1) % LIN_KEY_DIM) < LIN_KEY_DIM // 2
    cos = cos_ref[...]
    sin = sin_ref[...]
    qk_scale = LIN_KEY_DIM ** -0.5

    def rotary(x):
        rot = jnp.where(first_half, -pltpu.roll(x, LIN_QK_W - LIN_KEY_DIM // 2, 1), pltpu.roll(x, LIN_KEY_DIM // 2, 1))
        return x * cos + rot * sin

    def group(g_idx, q, k, v, lg, gate, gain):
        b = _ldot3(tril_b, lg)
        b_end = _ldot3(same_b, lg)
        q_dec = q * jnp.exp(b)
        k_inc = (k * jnp.exp(-b)).astype(BF16)
        k_dec = k * jnp.exp(b_end - b)
        b_end_t = b_end.T
        vb = v.astype(BF16)
        o = jnp.zeros((tb, LIN_V_W), F32)
        for h in range(N_LIN_HEADS):
            qh = jnp.where(head_qk == h, q_dec, 0.0).astype(BF16)
            sc = lax.dot_general(qh, k_inc, _NT, preferred_element_type=F32)
            p = jnp.where(tril, sc, 0.0).astype(BF16)
            o = o + jnp.where(head_v == h, _dot(p, vb), 0.0)
        s = s_scr[g_idx]
        inter = []
        for i in range(n_sub):
            lo, hi = i * c, (i + 1) * c
            inter.append(_dot(q_dec[lo:hi].astype(BF16), s.astype(BF16)))
            u = lax.dot_general(k_dec[lo:hi].astype(BF16), v[lo:hi].astype(BF16), _TN, preferred_element_type=F32)
            s = jnp.exp(b_end_t[:, lo:lo + 1]) * s + jnp.where(blockdiag, u, 0.0)
        s_scr[g_idx] = s
        o = o + (inter[0] if n_sub == 1 else jnp.concatenate(inter, axis=0))
        ms = _rdot3(o * o, head_mean)
        on = o * lax.rsqrt(ms + RMS_EPS) * gain
        return on * (gate / (1.0 + jnp.exp(-gate)))

    a = LIN_QK_W
    w = LIN_V_W
    rq = rotary(tok_ref[0, :, 0:a])
    rk = rotary(tok_ref[0, :, a:2 * a]) * qk_scale
    rv = tok_ref[0, :, 2 * a:2 * a + w]
    rgate = tok_ref[0, :, 2 * a + w:2 * a + 2 * w]
    ret_lg = jnp.broadcast_to(rdec_ref[...], (tb, LIN_QK_W))
    o_ref[0, :, 0:w] = group(0, rq, rk, rv, ret_lg, rgate, rgain_ref[...]).astype(o_ref.dtype)

    base = 2 * a + 2 * w
    gq = tok_ref[0, :, base:base + a] * qk_scale
    gk = tok_ref[0, :, base + a:base + 2 * a]
    gv = tok_ref[0, :, base + 2 * a:base + 2 * a + w]
    ggate = tok_ref[0, :, base + 2 * a + w:base + 2 * a + 2 * w]
    ga = tok_ref[0, :, LIN_TOK_W:TOK_W]
    z = _dot(ga.astype(BF16), w2_ref[...]) + gb_ref[...]
    gla_lg = (jnp.minimum(z, 0.0) - jnp.log(1.0 + jnp.exp(-jnp.abs(z)))) * (1.0 / GLA_TAU)
    o_ref[0, :, w:2 * w] = group(1, gq, gk, gv, gla_lg, ggate, ggain_ref[...]).astype(o_ref.dtype)

    @pl.when(t == pl.num_programs(1) - 1)
    def _():
        sfin_ref[0] = s_scr[...]


def _linrec(tok, cos, sin, w2p, gb, rdec, rgain, ggain, s0, tb, c):
    nb, t, _ = tok.shape
    full = lambda s: pl.BlockSpec(s, lambda b, i: (0,) * len(s))
    st = pl.BlockSpec((1, 2, LIN_QK_W, LIN_V_W), lambda b, i: (b, 0, 0, 0))
    return pl.pallas_call(
        functools.partial(_linrec_kernel, tb=tb, c=c),
        grid=(nb, t // tb),
        in_specs=[pl.BlockSpec((1, tb, TOK_W), lambda b, i: (b, i, 0)),
                  pl.BlockSpec((tb, LIN_QK_W), lambda b, i: (i, 0)),
                  pl.BlockSpec((tb, LIN_QK_W), lambda b, i: (i, 0)),
                  full((GA_PAD, LIN_QK_W)), full((1, LIN_QK_W)), full((1, LIN_QK_W)),
                  full((1, LIN_V_W)), full((1, LIN_V_W)), st],
        out_specs=[pl.BlockSpec((1, tb, 2 * LIN_V_W), lambda b, i: (b, i, 0)), st],
        out_shape=[jax.ShapeDtypeStruct((nb, t, 2 * LIN_V_W), BF16),
                   jax.ShapeDtypeStruct((nb, 2, LIN_QK_W, LIN_V_W), F32)],
        scratch_shapes=[pltpu.VMEM((2, LIN_QK_W, LIN_V_W), F32)],
        compiler_params=_cparams("parallel", "arbitrary"),
        name="linrec",
    )(tok, cos, sin, w2p, gb, rdec, rgain, ggain, s0)


def _select_topk(gate_t, n_valid, nbp, width):
    rown = lax.broadcasted_iota(jnp.int32, (nbp, width), 0)
    rown_f = rown.astype(F32)
    valid = rown < n_valid
    g = jnp.where(valid, gate_t, -jnp.inf)
    sel = jnp.zeros((nbp, width), F32)
    for _ in range(MOBA_TOPK):
        mx = jnp.max(g, axis=0, keepdims=True)
        idx = jnp.min(jnp.where(g == mx, rown_f, float(nbp)), axis=0, keepdims=True)
        pick = rown_f == idx
        sel = jnp.where(pick, jnp.where(valid, 1.0, 0.0), sel)
        g = jnp.where(pick, -jnp.inf, g)
    return sel, rown


def _moba_prompt_kernel(bias_ref, q_ref, k_ref, v_ref, bucket_ref, o_ref,
                        kaug_scr, vaug_scr, kmean_scr, near_scr, s_scr, p_scr, s2_scr, *, nb, nbp):
    hp = pl.program_id(1)
    qi = pl.program_id(2)
    blk = MOBA_BLOCK
    n_tail = HEAD_DIM - nbp
    c_far, c_dummy = nbp, nbp + 2

    @pl.when(qi == 0)
    def _():
        lane = lax.broadcasted_iota(jnp.int32, (blk, HEAD_DIM), 1)
        lane_aug = lax.broadcasted_iota(jnp.int32, (blk, 2 * HEAD_DIM), 1)
        ones_row = jnp.where(lax.broadcasted_iota(jnp.int32, (V_ROWS - HEAD_DIM, blk), 0) == 0, 1.0, 0.0).astype(BF16)
        for hh in range(2):
            head = hp * 2 + hh
            rows = slice(hh * HEAD_DIM, (hh + 1) * HEAD_DIM)
            for d0 in (0, (nb + 1) * blk, (nb + 2) * blk):
                kaug_scr[hh, d0:d0 + blk, :] = jnp.where(lane_aug == HEAD_DIM + c_dummy, 1.0, 0.0).astype(BF16)
                vaug_scr[hh, 0:HEAD_DIM, d0:d0 + blk] = jnp.zeros((HEAD_DIM, blk), BF16)
                vaug_scr[hh, HEAD_DIM:V_ROWS, d0:d0 + blk] = ones_row
            kmean_scr[hh] = jnp.zeros((nbp, HEAD_DIM), F32)

            def fill(n, carry):
                off = pl.multiple_of(n * blk, blk)
                dst = pl.ds(pl.multiple_of((n + 1) * blk, blk), blk)
                kb = k_ref[0, rows, pl.ds(off, blk)].T
                kmean_scr[hh, pl.ds(n, 1), :] = jnp.mean(kb, axis=0, keepdims=True)
                extra = jnp.where((lane == n) | (lane == c_far) | (lane == c_far + 1), 1.0, 0.0)
                kaug_scr[hh, dst, :] = jnp.concatenate([kb, extra], axis=1).astype(BF16)
                vaug_scr[hh, 0:HEAD_DIM, dst] = v_ref[0, rows, pl.ds(off, blk)].astype(BF16)
                vaug_scr[hh, HEAD_DIM:V_ROWS, dst] = ones_row
                return carry

            lax.fori_loop(0, nb, fill, 0)

            far_b = bias_ref[head * N_BUCKETS + N_BUCKETS - 1]
            for c0 in range(0, 2 * blk, 128):
                bk = bucket_ref[c0:c0 + 128, :]

                def pick(bi, tab):
                    return jnp.where(bk == bi, (bias_ref[head * N_BUCKETS + bi] - far_b) * LOG2E, tab)

                near_scr[hh, c0:c0 + 128, :] = lax.fori_loop(0, N_BUCKETS, pick, jnp.where(bk < 0, MASK_NEG, 0.0))

    tail_row = lax.broadcasted_iota(jnp.int32, (n_tail, blk), 0)
    near_off = pl.multiple_of(qi * blk, blk)
    state = []
    q_far = []
    for hh in range(2):
        head = hp * 2 + hh
        q_t = q_ref[0, hh * HEAD_DIM:(hh + 1) * HEAD_DIM, :]
        gate_t = jnp.dot(kmean_scr[hh], q_t, preferred_element_type=F32, precision=lax.Precision.HIGHEST)
        sel, rown = _select_topk(gate_t, qi, nbp, blk)
        near_ok = (rown == qi) | ((rown == qi - 1) & (sel > 0.0))
        far_ok = (rown < qi - 1) & (sel > 0.0)
        far_v = jnp.full((n_tail, blk), bias_ref[head * N_BUCKETS + N_BUCKETS - 1] * LOG2E, F32)
        far_hi = far_v.astype(BF16).astype(F32)
        tail = jnp.where(tail_row == 0, far_hi, jnp.where(tail_row == 1, far_v - far_hi,
                                                          jnp.where(tail_row == 2, MASK_NEG, 0.0)))
        qs = q_t * (HEAD_DIM ** -0.5 * LOG2E)
        q_near = jnp.concatenate([qs, jnp.where(near_ok, 0.0, MASK_NEG), tail], axis=0).astype(BF16)
        q_far.append(jnp.concatenate([qs, jnp.where(far_ok, 0.0, MASK_NEG), tail], axis=0).astype(BF16))
        s_scr[hh] = _dot(kaug_scr[hh, pl.ds(near_off, 2 * blk), :], q_near) + near_scr[hh]
    for hh in range(2):
        m = jnp.max(s_scr[hh], axis=0, keepdims=True)
        p_scr[hh] = jnp.exp2(s_scr[hh] - m).astype(BF16)
        state += [m, _dot(vaug_scr[hh, :, pl.ds(near_off, 2 * blk)], p_scr[hh])]

    ck = 64
    n_ck = 2 * blk // ck
    n_it = (qi + 1) // 2

    def qk_tile(t, buf):
        off = pl.multiple_of(t * 2 * blk, 2 * blk)
        maxes = []
        for hh in range(2):
            mloc = None
            for c in range(n_ck):
                sc = _dot(kaug_scr[hh, pl.ds(off + c * ck, ck), :], q_far[hh])
                s2_scr[buf, hh, c * ck:(c + 1) * ck, :] = sc
                cm = jnp.max(sc.reshape(ck // 8, 8, blk), axis=0)
                mloc = cm if mloc is None else jnp.maximum(mloc, cm)
            maxes.append(jnp.max(mloc, axis=0, keepdims=True))
        return maxes

    def soft_pv(t, buf, maxes, st):
        off = pl.multiple_of(t * 2 * blk, 2 * blk)
        out = []
        for hh in range(2):
            m, acc = st[2 * hh], st[2 * hh + 1]
            m_new = jnp.maximum(m, maxes[hh])
            for c in range(n_ck):
                rows_c = slice(c * ck, (c + 1) * ck)
                p_scr[hh, rows_c, :] = jnp.exp2(s2_scr[buf, hh, rows_c, :] - m_new).astype(BF16)
            acc = acc * jnp.exp2(m - m_new) + _dot(vaug_scr[hh, :, pl.ds(off, 2 * blk)], p_scr[hh])
            out += [m_new, acc]
        return out

    def far_pair(j, carry):
        st, max_a = list(carry[:4]), list(carry[4:])
        max_b = qk_tile(2 * j + 1, 1)
        st = soft_pv(2 * j, 0, max_a, st)
        max_a = qk_tile(jnp.minimum(2 * j + 2, n_it), 0)
        st = soft_pv(2 * j + 1, 1, max_b, st)
        return tuple(st + max_a)

    carry = lax.fori_loop(0, (n_it + 1) // 2, far_pair, tuple(state + qk_tile(0, 0)))
    state = carry[:4]
    for hh in range(2):
        acc = state[2 * hh + 1]
        o = acc[0:HEAD_DIM] / acc[HEAD_DIM:HEAD_DIM + 1]
        o_ref[0, :, hh * HEAD_DIM:(hh + 1) * HEAD_DIM] = o.T.astype(o_ref.dtype)


def _moba_prompt(q_t, k_t, v_t, bias_flat, bucket_near):
    nbat, _, t = q_t.shape
    nb = t // MOBA_BLOCK
    nbp = max(8, -(-nb // 8) * 8)
    assert t % MOBA_BLOCK == 0 and nbp + 3 <= HEAD_DIM
    kv = pl.BlockSpec((1, 2 * HEAD_DIM, t), lambda b, hp, qi, bias: (b, hp, 0))
    return pl.pallas_call(
        functools.partial(_moba_prompt_kernel, nb=nb, nbp=nbp),
        grid_spec=pltpu.PrefetchScalarGridSpec(
            num_scalar_prefetch=1,
            grid=(nbat, H_MOBA // 2, nb),
            in_specs=[pl.BlockSpec((1, 2 * HEAD_DIM, MOBA_BLOCK), lambda b, hp, qi, bias: (b, hp, qi)), kv, kv,
                      pl.BlockSpec((2 * MOBA_BLOCK, MOBA_BLOCK), lambda b, hp, qi, bias: (0, 0))],
            out_specs=pl.BlockSpec((1, MOBA_BLOCK, 2 * HEAD_DIM), lambda b, hp, qi, bias: (b, qi, hp)),
            scratch_shapes=[pltpu.VMEM((2, t + 3 * MOBA_BLOCK, 2 * HEAD_DIM), BF16),
                            pltpu.VMEM((2, V_ROWS, t + 3 * MOBA_BLOCK), BF16),
                            pltpu.VMEM((2, nbp, HEAD_DIM), F32),
                            pltpu.VMEM((2, 2 * MOBA_BLOCK, MOBA_BLOCK), F32),
                            pltpu.VMEM((2, 2 * MOBA_BLOCK, MOBA_BLOCK), F32),
                            pltpu.VMEM((2, 2 * MOBA_BLOCK, MOBA_BLOCK), BF16),
                            pltpu.VMEM((2, 2, 2 * MOBA_BLOCK, MOBA_BLOCK), F32)],
        ),
        out_shape=jax.ShapeDtypeStruct((nbat, t, MOBA_W), BF16),
        compiler_params=_cparams("parallel", "parallel", "arbitrary"),
        name="moba_prompt",
    )(bias_flat, q_t, k_t, v_t, bucket_near)


N_PAGE_BUF = 32


def _sample_select_kernel(pt_ref, q_ref, cache_ref, sel_ref, buf, sem, km_scr, *, layer, n_pages, pages_per_block, n_buf):
    b = pl.program_id(0)
    nblk = n_pages // pages_per_block
    page = cache_ref.shape[-1]

    def copy(p, slot):
        return pltpu.make_async_copy(cache_ref.at[layer, pt_ref[b, p]], buf.at[slot], sem.at[slot])

    for p in range(n_buf):
        copy(p, p).start()
    km_scr[...] = jnp.zeros(km_scr.shape, F32)

    def block(j, carry):
        tot = None
        for u in range(pages_per_block):
            p = j * pages_per_block + u
            slot = p % n_buf
            copy(p, slot).wait()
            x = buf[slot].reshape(MOBA_W, page)
            tot = x if tot is None else tot + x

            @pl.when(p + n_buf < n_pages)
            def _():
                copy(p + n_buf, slot).start()

        km_scr[pl.ds(j, 1), :] = jnp.sum(tot.T, axis=0, keepdims=True) * (1.0 / (pages_per_block * page))
        return carry

    lax.fori_loop(0, nblk, block, 0)

    km = km_scr[...]
    nrow = km.shape[0]
    head_sum = (lax.broadcasted_iota(jnp.int32, (MOBA_W, 128), 0) // HEAD_DIM
                == lax.broadcasted_iota(jnp.int32, (MOBA_W, 128), 1))
    head_sum = jnp.where(head_sum, 1.0, 0.0).astype(BF16)
    rown = lax.broadcasted_iota(jnp.int32, (nrow, 128), 0)
    rown_f = rown.astype(F32)
    sel_ref[0] = jnp.zeros(sel_ref.shape[1:], jnp.int32)
    for t in range(q_ref.shape[1]):
        gate = _rdot3(km * q_ref[0, t:t + 1, :], head_sum)
        g = jnp.where(rown < nblk, gate, -jnp.inf)
        for r in range(MOBA_TOPK):
            mx = jnp.max(g, axis=0, keepdims=True)
            idx = jnp.min(jnp.where(g == mx, rown_f, float(nrow)), axis=0, keepdims=True)
            sel_ref[0, 4 * t + r:4 * t + r + 1, :] = idx.astype(jnp.int32)
            g = jnp.where(rown_f == idx, -jnp.inf, g)


def _sample_select(page_table, q_tb, cache_t, layer):
    nbat, n_pages = page_table.shape
    n_t = q_tb.shape[1]
    page = cache_t.shape[-1]
    ppb = MOBA_BLOCK // page
    nblk = n_pages // ppb
    assert n_pages % ppb == 0
    n_buf = min(N_PAGE_BUF, n_pages)
    sel_rows = -(-4 * n_t // 8) * 8
    return pl.pallas_call(
        functools.partial(_sample_select_kernel, layer=layer, n_pages=n_pages, pages_per_block=ppb, n_buf=n_buf),
        grid_spec=pltpu.PrefetchScalarGridSpec(
            num_scalar_prefetch=1,
            grid=(nbat,),
            in_specs=[pl.BlockSpec((1, n_t, MOBA_W), lambda b, pt: (b, 0, 0)),
                      pl.BlockSpec(memory_space=pl.ANY)],
            out_specs=pl.BlockSpec((1, sel_rows, 128), lambda b, pt: (b, 0, 0)),
            scratch_shapes=[pltpu.VMEM((n_buf, H_MOBA, HEAD_DIM, page), F32),
                            pltpu.SemaphoreType.DMA((n_buf,)),
                            pltpu.VMEM((-(-nblk // 8) * 8, MOBA_W), F32)],
        ),
        out_shape=jax.ShapeDtypeStruct((nbat, sel_rows, 128), jnp.int32),
        compiler_params=_cparams("arbitrary"),
        name="sample_select",
    )(page_table, q_tb, cache_t)


def _sample_attend_kernel(pid_ref, sel_ref, q_ref, kn_ref, vn_ref, bias_ref, ck_ref, cv_ref, o_ref,
                          kbuf, vbuf, sem, *, layer, n_t, pages_per_block, nblk):
    b = pl.program_id(0)
    nbat = pl.num_programs(0)
    page = ck_ref.shape[-1]
    n_slot = n_t * MOBA_TOPK

    def copies(bb, h, par):
        out = []
        for slot in range(n_slot):
            for u in range(pages_per_block):
                pid = pid_ref[bb, (h * n_slot + slot) * pages_per_block + u]
                lanes = pl.ds(u * page, page)
                out.append(pltpu.make_async_copy(ck_ref.at[layer, pid, h], kbuf.at[par, slot, :, lanes], sem.at[par, 0]))
                out.append(pltpu.make_async_copy(cv_ref.at[layer, pid, h], vbuf.at[par, slot, :, lanes], sem.at[par, 1]))
        return out

    @pl.when(b == 0)
    def _():
        for cp in copies(b, 0, 0):
            cp.start()

    key_off = lax.broadcasted_iota(jnp.int32, (1, n_t), 1)
    for h in range(H_MOBA):
        par = h % 2
        if h + 1 < H_MOBA:
            for cp in copies(b, h + 1, 1 - par):
                cp.start()
        else:
            @pl.when(b + 1 < nbat)
            def _():
                for cp in copies(b + 1, 0, 1 - par):
                    cp.start()

        for cp in copies(b, h, par):
            cp.wait()
        rows = slice(h * HEAD_DIM, (h + 1) * HEAD_DIM)
        qh = q_ref[0, rows, :] * (HEAD_DIM ** -0.5)
        kn = kn_ref[0, rows, :]
        vn = vn_ref[0, rows, :]
        for t in range(n_t):
            qb = jnp.broadcast_to(qh[:, t:t + 1], (HEAD_DIM, MOBA_BLOCK))
            s_own = jnp.sum(qh[:, t:t + 1] * kn, axis=0, keepdims=True) + bias_ref[h, n_t + 1 + t:n_t + 2 + t, 0:n_t]
            s_own = jnp.where(key_off <= t, s_own, MASK_NEG)
            m = jnp.max(s_own, axis=1, keepdims=True)
            ss = []
            for r in range(MOBA_TOPK):
                slot = t * MOBA_TOPK + r
                j = sel_ref[b, h * n_slot + slot]
                prod = qb * kbuf[par, slot]
                s = jnp.sum(prod.reshape(HEAD_DIM // 8, 8, MOBA_BLOCK).sum(axis=0), axis=0, keepdims=True)
                s = s + jnp.where(j == nblk - 1, bias_ref[h, t:t + 1, :], bias_ref[h, n_t:n_t + 1, :])
                m = jnp.maximum(m, jnp.max(s, axis=1, keepdims=True))
                ss.append(s)
            p_own = jnp.exp(s_own - m)
            l = jnp.sum(p_own, axis=1, keepdims=True)
            wsum = None
            for r, s in enumerate(ss):
                p = jnp.exp(s - m)
                l = l + jnp.sum(p, axis=1, keepdims=True)
                term = p * vbuf[par, t * MOBA_TOPK + r]
                wsum = term if wsum is None else wsum + term
            acc = jnp.sum(wsum, axis=1, keepdims=True) + jnp.sum(p_own * vn, axis=1, keepdims=True)
            o_ref[0, rows, t:t + 1] = acc / l


def _sample_attend(pids, sel, q_bt, kn_bt, vn_bt, bias_rows, cache_kt, cache_vt, layer, nblk):
    nbat = pids.shape[0]
    n_t = q_bt.shape[-1]
    page = cache_kt.shape[-1]
    ppb = MOBA_BLOCK // page
    n_slot = n_t * MOBA_TOPK
    new = pl.BlockSpec((1, MOBA_W, n_t), lambda b, pd, sl: (b, 0, 0))
    return pl.pallas_call(
        functools.partial(_sample_attend_kernel, layer=layer, n_t=n_t, pages_per_block=ppb, nblk=nblk),
        grid_spec=pltpu.PrefetchScalarGridSpec(
            num_scalar_prefetch=2,
            grid=(nbat,),
            in_specs=[new, new, new,
                      pl.BlockSpec(bias_rows.shape, lambda b, pd, sl: (0, 0, 0)),
                      pl.BlockSpec(memory_space=pl.ANY), pl.BlockSpec(memory_space=pl.ANY)],
            out_specs=new,
            scratch_shapes=[pltpu.VMEM((2, n_slot, HEAD_DIM, MOBA_BLOCK), F32),
                            pltpu.VMEM((2, n_slot, HEAD_DIM, MOBA_BLOCK), F32),
                            pltpu.SemaphoreType.DMA((2, 2))],
        ),
        out_shape=jax.ShapeDtypeStruct((nbat, MOBA_W, n_t), F32),
        compiler_params=_cparams("arbitrary"),
        name="sample_attend",
    )(pids, sel, q_bt, kn_bt, vn_bt, bias_rows, cache_kt, cache_vt)


def _outproj_kernel(x_ref, olin_ref, omoba_ref, wl_ref, wm_ref, g_ref, y_ref):
    mix = _dot(olin_ref[...], wl_ref[...]) + _dot(omoba_ref[...], wm_ref[...])
    y_ref[...] = x_ref[...] + _rms(mix, g_ref[...])


def _outproj(x, olin, omoba, w_lin, w_moba, g, tm):
    n, d = x.shape
    full = lambda s: pl.BlockSpec(s, lambda i: (0,) * len(s))
    rows = lambda w: pl.BlockSpec((tm, w), lambda i: (i, 0))
    return pl.pallas_call(
        _outproj_kernel,
        grid=(n // tm,),
        in_specs=[rows(d), rows(olin.shape[1]), rows(omoba.shape[1]), full(w_lin.shape), full(w_moba.shape), full((1, d))],
        out_specs=rows(d),
        out_shape=jax.ShapeDtypeStruct((n, d), F32),
        compiler_params=_cparams("parallel"),
        name="outproj",
    )(x, olin, omoba, w_lin, w_moba, g)


def _ffn_kernel(x_ref, gpre_ref, wg_ref, wu_ref, wd_ref, gpost_ref, y_ref, h_scr, acc_scr):
    j = pl.program_id(1)

    @pl.when(j == 0)
    def _():
        h_scr[...] = _rms(x_ref[...], gpre_ref[...]).astype(BF16)
        acc_scr[...] = jnp.zeros(acc_scr.shape, F32)

    h = h_scr[...]
    gate = _dot(h, wg_ref[...])
    up = _dot(h, wu_ref[...])
    act = (gate / (1.0 + jnp.exp(-gate)) * up).astype(BF16)
    acc_scr[...] += _dot(act, wd_ref[...])

    @pl.when(j == pl.num_programs(1) - 1)
    def _():
        y_ref[...] = x_ref[...] + _rms(acc_scr[...], gpost_ref[...])


def _ffn(x, gpre, wg, wu, wd, gpost, tm, fc):
    n, d = x.shape
    dff = wg.shape[1]
    rows = pl.BlockSpec((tm, d), lambda i, j: (i, 0))
    vec = pl.BlockSpec((1, d), lambda i, j: (0, 0))
    return pl.pallas_call(
        _ffn_kernel,
        grid=(n // tm, dff // fc),
        in_specs=[rows, vec, pl.BlockSpec((d, fc), lambda i, j: (0, j)), pl.BlockSpec((d, fc), lambda i, j: (0, j)),
                  pl.BlockSpec((fc, d), lambda i, j: (j, 0)), vec],
        out_specs=rows,
        out_shape=jax.ShapeDtypeStruct((n, d), F32),
        scratch_shapes=[pltpu.VMEM((tm, d), BF16), pltpu.VMEM((tm, d), F32)],
        compiler_params=_cparams("parallel", "arbitrary"),
        name="ffn",
    )(x, gpre, wg, wu, wd, gpost)


def _t5_bucket_np(dist):
    n = np.maximum(dist, 0)
    max_exact = N_BUCKETS // 2
    nf = np.maximum(n, 1).astype(np.float32)
    large = max_exact + (np.log(nf / np.float32(max_exact)) / np.float32(math.log(MAX_DISTANCE / max_exact))
                         * np.float32(N_BUCKETS - max_exact)).astype(np.int32)
    large = np.minimum(large, N_BUCKETS - 1)
    return np.where(n < max_exact, n, large).astype(np.int32)


def _rotary_tables(pos):
    half = LIN_KEY_DIM // 2
    inv = 1.0 / (ROPE_BASE ** jnp.linspace(0.0, 1.0, half, dtype=F32))
    ang = pos.astype(F32)[:, None] * inv[None, :]
    tile = lambda a: jnp.tile(a, (1, 2 * N_LIN_HEADS))
    return tile(jnp.cos(ang)), tile(jnp.sin(ang))


def _blockdiag_state(s):
    eye = jnp.eye(N_LIN_HEADS, dtype=s.dtype)
    return (s[:, :, :, None, :] * eye[None, :, None, :, None]).reshape(s.shape[0], LIN_QK_W, LIN_V_W)


def _head_states(sb):
    x = sb.reshape(sb.shape[0], N_LIN_HEADS, LIN_KEY_DIM, N_LIN_HEADS, HEAD_DIM)
    return jnp.stack([x[:, h, :, h, :] for h in range(N_LIN_HEADS)], axis=1)


def _row_tile(n, want):
    t = min(n, want)
    while n % t:
        t //= 2
    return t


def kernel(x_prompt, x_sample, cache_k, cache_v, state_ret, state_gla, page_table, w_in, w_out, gla_gate_w2, gla_gate_b, ret_norm_g, gla_norm_g, norm_attn_pre, norm_attn_post, norm_ffn_pre, norm_ffn_post, w_ffn_gate, w_ffn_up, w_ffn_down, rel_bias):
    depth = w_in.shape[0]
    nbp_, t_p, d_model = x_prompt.shape
    nbs, t_s, _ = x_sample.shape
    n_pages = page_table.shape[1]
    page = cache_k.shape[2]
    past_len = n_pages * page
    assert past_len % MOBA_BLOCK == 0 and t_s <= MOBA_BLOCK and t_p % MOBA_BLOCK == 0
    dff = w_ffn_gate.shape[2]

    cos_p, sin_p = _rotary_tables(jnp.arange(t_p, dtype=jnp.int32))
    cos_s, sin_s = _rotary_tables(past_len + jnp.arange(t_s, dtype=jnp.int32))
    off = np.arange(MOBA_BLOCK)
    dist_own = off[None, :] - off[:, None]
    bucket_near = np.concatenate([_t5_bucket_np(MOBA_BLOCK + dist_own),
                                  np.where(dist_own >= 0, _t5_bucket_np(dist_own), -1)], axis=0).astype(np.int32)
    bias_h = rel_bias.astype(F32).T
    bias_flat = bias_h.reshape(-1)
    far = bias_h[:, N_BUCKETS - 1]
    prev_rows = bias_h[:, _t5_bucket_np(MOBA_BLOCK + np.arange(t_s)[:, None] - off[None, :])]
    far_row = jnp.broadcast_to(far[:, None, None], (H_MOBA, 1, MOBA_BLOCK))
    own_rows = bias_h[:, _t5_bucket_np(np.arange(t_s)[:, None] - off[None, :])]
    n_rows = 2 * t_s + 1
    pad_rows = -(-n_rows // 8) * 8 - n_rows
    bias_rows = jnp.concatenate([prev_rows, far_row, own_rows, jnp.zeros((H_MOBA, pad_rows, MOBA_BLOCK), F32)], axis=1)

    rdec = jnp.repeat(jnp.log(1.0 - 2.0 ** (-5.0 - jnp.arange(N_LIN_HEADS, dtype=F32))), LIN_KEY_DIM)[None, :]
    cache_kt = jnp.transpose(cache_k, (0, 1, 3, 4, 2))
    cache_vt = jnp.transpose(cache_v, (0, 1, 3, 4, 2))
    w_in_t = jnp.swapaxes(w_in, 1, 2)

    hp = x_prompt
    hs = x_sample.reshape(1, nbs * t_s, d_model)
    zero_state = jnp.zeros((nbp_, 2, LIN_QK_W, LIN_V_W), F32)
    tm_p = _row_tile(t_p, 512)
    tb_p = _row_tile(t_p, 128)
    c_p = math.gcd(t_p, REC_CHUNK)
    c_s = math.gcd(t_s, REC_CHUNK)
    fc = dff // 2 if (dff // 2) % 128 == 0 else dff
    outs = {k: [] for k in ("kp", "vp", "rp", "gp", "ks", "vs", "rs", "gs")}

    for l in range(depth):
        seg = w_in_t[l]
        wtok = jnp.concatenate([seg[:LIN_TOK_W + GLA_RANK], jnp.zeros((GA_PAD - GLA_RANK, d_model), F32)], axis=0).T.astype(BF16)
        m0 = LIN_TOK_W + GLA_RANK
        wq_t = seg[m0:m0 + MOBA_W].astype(BF16)
        wk_t = seg[m0 + MOBA_W:m0 + 2 * MOBA_W].astype(BF16)
        wv_t = seg[m0 + 2 * MOBA_W:m0 + 3 * MOBA_W].astype(BF16)
        w2p = jnp.concatenate([gla_gate_w2[l], jnp.zeros((GA_PAD - GLA_RANK, LIN_QK_W), F32)], axis=0).astype(BF16)
        gb = gla_gate_b[l][None, :]
        rgain = jnp.tile(ret_norm_g[l], N_LIN_HEADS)[None, :]
        ggain = jnp.tile(gla_norm_g[l], N_LIN_HEADS)[None, :]
        w_o = w_out[l].astype(BF16)
        w_o_lin, w_o_moba = w_o[:2 * LIN_V_W], w_o[2 * LIN_V_W:]
        wg, wu, wd = w_ffn_gate[l].astype(BF16), w_ffn_up[l].astype(BF16), w_ffn_down[l].astype(BF16)
        g_pre, g_post = norm_attn_pre[l][None, :], norm_attn_post[l][None, :]
        f_pre, f_post = norm_ffn_pre[l][None, :], norm_ffn_post[l][None, :]

        tok, q_t, k_t, v_t = _inproj(hp, g_pre, wtok, wq_t, wk_t, wv_t, tm_p)
        olin, sfin = _linrec(tok, cos_p, sin_p, w2p, gb, rdec, rgain, ggain, zero_state, tb_p, c_p)
        omoba = _moba_prompt(q_t, k_t, v_t, bias_flat, bucket_near)
        x1 = _outproj(hp.reshape(-1, d_model), olin.reshape(-1, 2 * LIN_V_W), omoba.reshape(-1, MOBA_W),
                      w_o_lin, w_o_moba, g_post, tm_p)
        hp = _ffn(x1, f_pre, wg, wu, wd, f_post, tm_p, fc).reshape(nbp_, t_p, d_model)
        outs["kp"].append(jnp.transpose(k_t.reshape(nbp_, H_MOBA, HEAD_DIM, t_p), (0, 3, 1, 2)))
        outs["vp"].append(jnp.transpose(v_t.reshape(nbp_, H_MOBA, HEAD_DIM, t_p), (0, 3, 1, 2)))
        outs["rp"].append(_head_states(sfin[:, 0]))
        outs["gp"].append(_head_states(sfin[:, 1]))

        n_s = nbs * t_s
        tok, q_t, k_t, v_t = _inproj(hs, g_pre, wtok, wq_t, wk_t, wv_t, n_s)
        s0 = jnp.stack([_blockdiag_state(state_ret[l]), _blockdiag_state(state_gla[l])], axis=1)
        olin, sfin = _linrec(tok.reshape(nbs, t_s, TOK_W), cos_s, sin_s, w2p, gb, rdec, rgain, ggain, s0, t_s, c_s)
        per_batch = lambda a: jnp.transpose(a.reshape(MOBA_W, nbs, t_s), (1, 0, 2))
        q_bt, k_bt, v_bt = per_batch(q_t), per_batch(k_t), per_batch(v_t)
        sel = _sample_select(page_table, jnp.transpose(q_bt, (0, 2, 1)), cache_kt, l)
        sel = jnp.transpose(sel[:, :4 * t_s, :H_MOBA].reshape(nbs, t_s, 4, H_MOBA)[:, :, :MOBA_TOPK], (0, 3, 1, 2))
        sel = sel.reshape(nbs, -1)
        ppb = MOBA_BLOCK // page
        logical = (sel[:, :, None] * ppb + jnp.arange(ppb, dtype=jnp.int32)[None, None, :]).reshape(nbs, -1)
        pids = jnp.take_along_axis(page_table, logical, axis=1)
        o_bt = _sample_attend(pids, sel, q_bt, k_bt, v_bt, bias_rows, cache_kt, cache_vt, l, n_pages // ppb)
        omoba = jnp.transpose(o_bt, (0, 2, 1)).reshape(n_s, MOBA_W).astype(BF16)
        x1 = _outproj(hs.reshape(n_s, d_model), olin.reshape(n_s, 2 * LIN_V_W), omoba, w_o_lin, w_o_moba, g_post, n_s)
        hs = _ffn(x1, f_pre, wg, wu, wd, f_post, n_s, fc).reshape(1, n_s, d_model)
        to_cache = lambda a: jnp.transpose(a.reshape(H_MOBA, HEAD_DIM, nbs, t_s), (2, 3, 0, 1))
        outs["ks"].append(to_cache(k_t))
        outs["vs"].append(to_cache(v_t))
        outs["rs"].append(_head_states(sfin[:, 0]))
        outs["gs"].append(_head_states(sfin[:, 1]))

    st = lambda k: jnp.stack(outs[k])
    return (hp, hs.reshape(nbs, t_s, d_model), st("kp"), st("vp"), st("rp"), st("gp"),
            st("ks"), st("vs"), st("rs"), st("gs"))
```

```python
import functools
import math

import jax
import jax.numpy as jnp
import numpy as np
from jax import lax
from jax.experimental import pallas as pl
from jax.experimental.pallas import tpu as pltpu

F32 = jnp.float32
BF16 = jnp.bfloat16

HEAD_DIM = 64
LIN_KEY_DIM = 32
N_LIN_HEADS = 4
H_MOBA = 8
LIN_QK_W = N_LIN_HEADS * LIN_KEY_DIM
LIN_V_W = N_LIN_HEADS * HEAD_DIM
MOBA_W = H_MOBA * HEAD_DIM
GLA_RANK = 16
GLA_TAU = 16.0
REC_CHUNK = 16
ROPE_BASE = 10000.0
MOBA_BLOCK = 256
MOBA_TOPK = 3
N_BUCKETS = 32
MAX_DISTANCE = 128
RMS_EPS = 1e-6
LIN_TOK_W = 2 * (2 * LIN_QK_W + 2 * LIN_V_W)
GA_PAD = 128
TOK_W = LIN_TOK_W + GA_PAD

VMEM_LIMIT_BYTES = 56 * 1024 * 1024
MASK_NEG = -1e30
LOG2E = 1.4426950408889634
V_ROWS = 80

_NT = (((1,), (1,)), ((), ()))
_TN = (((0,), (0,)), ((), ()))


def _cparams(*sem):
    return pltpu.CompilerParams(dimension_semantics=sem, vmem_limit_bytes=VMEM_LIMIT_BYTES)


def _rms(x, g):
    return x * lax.rsqrt(jnp.mean(x * x, axis=-1, keepdims=True) + RMS_EPS) * g


def _split3(x):
    hi = x.astype(BF16)
    r = x - hi.astype(F32)
    mid = r.astype(BF16)
    lo = (r - mid.astype(F32)).astype(BF16)
    return hi, mid, lo


def _dot(a, b):
    return jnp.dot(a, b, preferred_element_type=F32)


def _ldot3(a_bf16, x):
    hi, mid, lo = _split3(x)
    return _dot(a_bf16, hi) + _dot(a_bf16, mid) + _dot(a_bf16, lo)


def _rdot3(x, b_bf16):
    hi, mid, lo = _split3(x)
    return _dot(hi, b_bf16) + _dot(mid, b_bf16) + _dot(lo, b_bf16)


def _inproj_kernel(x_ref, g_ref, wtok_ref, wq_ref, wk_ref, wv_ref, tok_ref, q_ref, k_ref, v_ref):
    hb = _rms(x_ref[0], g_ref[...]).astype(BF16)
    tok_ref[0] = _dot(hb, wtok_ref[...])
    q_ref[0] = lax.dot_general(wq_ref[...], hb, _NT, preferred_element_type=F32)
    k_ref[0] = lax.dot_general(wk_ref[...], hb, _NT, preferred_element_type=F32)
    v_ref[0] = lax.dot_general(wv_ref[...], hb, _NT, preferred_element_type=F32)


def _inproj(x, g, wtok, wq_t, wk_t, wv_t, tm):
    nb, t, d = x.shape
    full = lambda s: pl.BlockSpec(s, lambda b, i: (0,) * len(s))
    chn = pl.BlockSpec((1, MOBA_W, tm), lambda b, i: (b, 0, i))
    return pl.pallas_call(
        _inproj_kernel,
        grid=(nb, t // tm),
        in_specs=[pl.BlockSpec((1, tm, d), lambda b, i: (b, i, 0)), full((1, d)), full((d, TOK_W)),
                  full((MOBA_W, d)), full((MOBA_W, d)), full((MOBA_W, d))],
        out_specs=[pl.BlockSpec((1, tm, TOK_W), lambda b, i: (b, i, 0)), chn, chn, chn],
        out_shape=[jax.ShapeDtypeStruct((nb, t, TOK_W), F32)] + [jax.ShapeDtypeStruct((nb, MOBA_W, t), F32)] * 3,
        compiler_params=_cparams("parallel", "parallel"),
        name="inproj",
    )(x, g, wtok, wq_t, wk_t, wv_t)


def _linrec_kernel(tok_ref, cos_ref, sin_ref, w2_ref, gb_ref, rdec_ref, rgain_ref, ggain_ref, s0_ref,
                   o_ref, sfin_ref, s_scr, *, tb, c, bpb):
    t = pl.program_id(1)

    @pl.when(t == 0)
    def _():
        s_scr[...] = s0_ref[...]

    n_sub = tb // c
    shift = int(math.log2(c))
    row = lax.broadcasted_iota(jnp.int32, (tb, tb), 0)
    col = lax.broadcasted_iota(jnp.int32, (tb, tb), 1)
    same = (row >> shift) == (col >> shift)
    tril = same & (col <= row)
    tril_b = jnp.where(tril, 1.0, 0.0).astype(BF16)
    same_b = jnp.where(same, 1.0, 0.0).astype(BF16)
    head_qk = lax.broadcasted_iota(jnp.int32, (1, LIN_QK_W), 1) // LIN_KEY_DIM
    head_v = lax.broadcasted_iota(jnp.int32, (1, LIN_V_W), 1) // HEAD_DIM
    blockdiag = (lax.broadcasted_iota(jnp.int32, (LIN_QK_W, LIN_V_W), 0) // LIN_KEY_DIM
                 == lax.broadcasted_iota(jnp.int32, (LIN_QK_W, LIN_V_W), 1) // HEAD_DIM)
    gi = lax.broadcasted_iota(jnp.int32, (LIN_V_W, LIN_V_W), 0) // HEAD_DIM
    gj = lax.broadcasted_iota(jnp.int32, (LIN_V_W, LIN_V_W), 1) // HEAD_DIM
    head_mean = jnp.where(gi == gj, 1.0 / HEAD_DIM, 0.0).astype(BF16)
    first_half = (lax.broadcasted_iota(jnp.int32, (1, LIN_QK_W), 1) % LIN_KEY_DIM) < LIN_KEY_DIM // 2
    cos = cos_ref[...]
    sin = sin_ref[...]
    qk_scale = LIN_KEY_DIM ** -0.5

    def rotary(x):
        rot = jnp.where(first_half, -pltpu.roll(x, LIN_QK_W - LIN_KEY_DIM // 2, 1), pltpu.roll(x, LIN_KEY_DIM // 2, 1))
        return x * cos + rot * sin

    sub_of_col = lax.broadcasted_iota(jnp.int32, (1, tb), 1) >> shift

    def group(bi, g_idx, q, k, v, b, b_end, gate, gain):
        q_dec = q * jnp.exp(b)
        k_inc = (k * jnp.exp(-b)).astype(BF16)
        k_dec = k * jnp.exp(b_end - b)
        b_end_t = b_end.T
        k_dec_t = k_dec.T
        vb = v.astype(BF16)
        q4 = jnp.concatenate([jnp.where(head_qk == h, q_dec, 0.0) for h in range(N_LIN_HEADS)], axis=0).astype(BF16)
        sc = lax.dot_general(q4, k_inc, _NT, preferred_element_type=F32)
        p4 = jnp.concatenate([jnp.where(tril, sc[h * tb:(h + 1) * tb], 0.0) for h in range(N_LIN_HEADS)], axis=0).astype(BF16)
        pv = _dot(p4, vb)
        o = jnp.where(head_v == 0, pv[0:tb], 0.0)
        for h in range(1, N_LIN_HEADS):
            o = o + jnp.where(head_v == h, pv[h * tb:(h + 1) * tb], 0.0)
        if n_sub == 1:
            u_all = lax.dot_general(k_dec.astype(BF16), vb, _TN, preferred_element_type=F32)
        else:
            k_stack = jnp.concatenate([jnp.where(sub_of_col == i, k_dec_t, 0.0) for i in range(n_sub)], axis=0)
            u_all = _dot(k_stack.astype(BF16), vb)
        s = s_scr[bi, g_idx]
        inter = []
        for i in range(n_sub):
            lo, hi = i * c, (i + 1) * c
            inter.append(_dot(q_dec[lo:hi].astype(BF16), s.astype(BF16)))
            u = u_all[i * LIN_QK_W:(i + 1) * LIN_QK_W]
            s = jnp.exp(b_end_t[:, lo:lo + 1]) * s + jnp.where(blockdiag, u, 0.0)
        s_scr[bi, g_idx] = s
        o = o + (inter[0] if n_sub == 1 else jnp.concatenate(inter, axis=0))
        ms = _rdot3(o * o, head_mean)
        on = o * lax.rsqrt(ms + RMS_EPS) * gain
        return on * (gate / (1.0 + jnp.exp(-gate)))

    a = LIN_QK_W
    w = LIN_V_W
    base = 2 * a + 2 * w
    for bi in range(bpb):
        rq = rotary(tok_ref[bi, :, 0:a])
        rk = rotary(tok_ref[bi, :, a:2 * a]) * qk_scale
        rv = tok_ref[bi, :, 2 * a:2 * a + w]
        rgate = tok_ref[bi, :, 2 * a + w:2 * a + 2 * w]
        pos_in_chunk = (lax.broadcasted_iota(jnp.int32, (tb, 1), 0) & (c - 1)).astype(F32)
        ret_b = (pos_in_chunk + 1.0) * rdec_ref[...]
        ret_b_end = jnp.broadcast_to(float(c) * rdec_ref[...], (tb, LIN_QK_W))
        o_ref[bi, :, 0:w] = group(bi, 0, rq, rk, rv, ret_b, ret_b_end, rgate, rgain_ref[...]).astype(o_ref.dtype)

        gq = tok_ref[bi, :, base:base + a] * qk_scale
        gk = tok_ref[bi, :, base + a:base + 2 * a]
        gv = tok_ref[bi, :, base + 2 * a:base + 2 * a + w]
        ggate = tok_ref[bi, :, base + 2 * a + w:base + 2 * a + 2 * w]
        ga = tok_ref[bi, :, LIN_TOK_W:TOK_W]
        z = _dot(ga.astype(BF16), w2_ref[...]) + gb_ref[...]
        gla_lg = (jnp.minimum(z, 0.0) - jnp.log(1.0 + jnp.exp(-jnp.abs(z)))) * (1.0 / GLA_TAU)
        gla_b = _ldot3(tril_b, gla_lg)
        gla_b_end = _ldot3(same_b, gla_lg)
        o_ref[bi, :, w:2 * w] = group(bi, 1, gq, gk, gv, gla_b, gla_b_end, ggate, ggain_ref[...]).astype(o_ref.dtype)

    @pl.when(t == pl.num_programs(1) - 1)
    def _():
        sfin_ref[...] = s_scr[...]


def _linrec(tok, cos, sin, w2p, gb, rdec, rgain, ggain, s0, tb, c):
    nb, t, _ = tok.shape
    bpb = 2 if nb % 2 == 0 else 1
    full = lambda s: pl.BlockSpec(s, lambda b, i: (0,) * len(s))
    st = pl.BlockSpec((bpb, 2, LIN_QK_W, LIN_V_W), lambda b, i: (b, 0, 0, 0))
    return pl.pallas_call(
        functools.partial(_linrec_kernel, tb=tb, c=c, bpb=bpb),
        grid=(nb // bpb, t // tb),
        in_specs=[pl.BlockSpec((bpb, tb, TOK_W), lambda b, i: (b, i, 0)),
                  pl.BlockSpec((tb, LIN_QK_W), lambda b, i: (i, 0)),
                  pl.BlockSpec((tb, LIN_QK_W), lambda b, i: (i, 0)),
                  full((GA_PAD, LIN_QK_W)), full((1, LIN_QK_W)), full((1, LIN_QK_W)),
                  full((1, LIN_V_W)), full((1, LIN_V_W)), st],
        out_specs=[pl.BlockSpec((bpb, tb, 2 * LIN_V_W), lambda b, i: (b, i, 0)), st],
        out_shape=[jax.ShapeDtypeStruct((nb, t, 2 * LIN_V_W), BF16),
                   jax.ShapeDtypeStruct((nb, 2, LIN_QK_W, LIN_V_W), F32)],
        scratch_shapes=[pltpu.VMEM((bpb, 2, LIN_QK_W, LIN_V_W), F32)],
        compiler_params=_cparams("parallel", "arbitrary"),
        name="linrec",
    )(tok, cos, sin, w2p, gb, rdec, rgain, ggain, s0)


def _select_topk(gate_t, n_valid, nbp, width):
    rown = lax.broadcasted_iota(jnp.int32, (nbp, width), 0)
    rown_f = rown.astype(F32)
    valid = rown < n_valid
    g = jnp.where(valid, gate_t, -jnp.inf)
    sel = jnp.zeros((nbp, width), F32)
    for _ in range(MOBA_TOPK):
        mx = jnp.max(g, axis=0, keepdims=True)
        idx = jnp.min(jnp.where(g == mx, rown_f, float(nbp)), axis=0, keepdims=True)
        pick = rown_f == idx
        sel = jnp.where(pick, jnp.where(valid, 1.0, 0.0), sel)
        g = jnp.where(pick, -jnp.inf, g)
    return sel, rown


def _moba_prompt_kernel(bias_ref, q_ref, k_ref, v_ref, bucket_ref, o_ref,
                        kaug_scr, vaug_scr, kmean_scr, near_scr, s_scr, p_scr, s2_scr, *, nb, nbp):
    hp = pl.program_id(1)
    qi = pl.program_id(2)
    blk = MOBA_BLOCK
    n_tail = HEAD_DIM - nbp
    c_far, c_dummy = nbp, nbp + 2

    @pl.when(qi == 0)
    def _():
        lane = lax.broadcasted_iota(jnp.int32, (blk, HEAD_DIM), 1)
        lane_aug = lax.broadcasted_iota(jnp.int32, (blk, 2 * HEAD_DIM), 1)
        ones_row = jnp.where(lax.broadcasted_iota(jnp.int32, (V_ROWS - HEAD_DIM, blk), 0) == 0, 1.0, 0.0).astype(BF16)
        for hh in range(2):
            head = hp * 2 + hh
            rows = slice(hh * HEAD_DIM, (hh + 1) * HEAD_DIM)
            for d0 in (0, (nb + 1) * blk, (nb + 2) * blk):
                kaug_scr[hh, d0:d0 + blk, :] = jnp.where(lane_aug == HEAD_DIM + c_dummy, 1.0, 0.0).astype(BF16)
                vaug_scr[hh, 0:HEAD_DIM, d0:d0 + blk] = jnp.zeros((HEAD_DIM, blk), BF16)
                vaug_scr[hh, HEAD_DIM:V_ROWS, d0:d0 + blk] = ones_row
            kmean_scr[hh] = jnp.zeros((nbp, HEAD_DIM), F32)

            def fill(n, carry):
                off = pl.multiple_of(n * blk, blk)
                dst = pl.ds(pl.multiple_of((n + 1) * blk, blk), blk)
                kb = k_ref[0, rows, pl.ds(off, blk)].T
                kmean_scr[hh, pl.ds(n, 1), :] = jnp.mean(kb, axis=0, keepdims=True)
                extra = jnp.where((lane == n) | (lane == c_far) | (lane == c_far + 1), 1.0, 0.0)
                kaug_scr[hh, dst, :] = jnp.concatenate([kb, extra], axis=1).astype(BF16)
                vaug_scr[hh, 0:HEAD_DIM, dst] = v_ref[0, rows, pl.ds(off, blk)].astype(BF16)
                vaug_scr[hh, HEAD_DIM:V_ROWS, dst] = ones_row
                return carry

            lax.fori_loop(0, nb, fill, 0)

            far_b = bias_ref[head * N_BUCKETS + N_BUCKETS - 1]
            for c0 in range(0, 2 * blk, 128):
                bk = bucket_ref[c0:c0 + 128, :]

                def pick(bi, tab):
                    return jnp.where(bk == bi, (bias_ref[head * N_BUCKETS + bi] - far_b) * LOG2E, tab)

                near_scr[hh, c0:c0 + 128, :] = lax.fori_loop(0, N_BUCKETS, pick, jnp.where(bk < 0, MASK_NEG, 0.0))

    tail_row = lax.broadcasted_iota(jnp.int32, (n_tail, blk), 0)
    near_off = pl.multiple_of(qi * blk, blk)
    state = []
    q_far = []
    for hh in range(2):
        head = hp * 2 + hh
        q_t = q_ref[0, hh * HEAD_DIM:(hh + 1) * HEAD_DIM, :]
        gate_t = jnp.dot(kmean_scr[hh], q_t, preferred_element_type=F32, precision=lax.Precision.HIGHEST)
        sel, rown = _select_topk(gate_t, qi, nbp, blk)
        near_ok = (rown == qi) | ((rown == qi - 1) & (sel > 0.0))
        far_ok = (rown < qi - 1) & (sel > 0.0)
        far_v = jnp.full((n_tail, blk), bias_ref[head * N_BUCKETS + N_BUCKETS - 1] * LOG2E, F32)
        far_hi = far_v.astype(BF16).astype(F32)
        tail = jnp.where(tail_row == 0, far_hi, jnp.where(tail_row == 1, far_v - far_hi,
                                                          jnp.where(tail_row == 2, MASK_NEG, 0.0)))
        qs = q_t * (HEAD_DIM ** -0.5 * LOG2E)
        q_near = jnp.concatenate([qs, jnp.where(near_ok, 0.0, MASK_NEG), tail], axis=0).astype(BF16)
        q_far.append(jnp.concatenate([qs, jnp.where(far_ok, 0.0, MASK_NEG), tail], axis=0).astype(BF16))
        s_scr[hh] = _dot(kaug_scr[hh, pl.ds(near_off, 2 * blk), :], q_near) + near_scr[hh]
    for hh in range(2):
        m = jnp.max(s_scr[hh], axis=0, keepdims=True)
        p_scr[hh] = jnp.exp2(s_scr[hh] - m).astype(BF16)
        state += [m, _dot(vaug_scr[hh, :, pl.ds(near_off, 2 * blk)], p_scr[hh])]

    ck = 64
    n_ck = 2 * blk // ck
    n_it = (qi + 1) // 2

    def qk_tile(t, buf):
        off = pl.multiple_of(t * 2 * blk, 2 * blk)
        maxes = []
        for hh in range(2):
            mloc = None
            for c in range(n_ck):
                sc = _dot(kaug_scr[hh, pl.ds(off + c * ck, ck), :], q_far[hh])
                s2_scr[buf, hh, c * ck:(c + 1) * ck, :] = sc
                cm = jnp.max(sc.reshape(ck // 8, 8, blk), axis=0)
                mloc = cm if mloc is None else jnp.maximum(mloc, cm)
            maxes.append(jnp.max(mloc, axis=0, keepdims=True))
        return maxes

    def soft_pv(t, buf, maxes, st):
        off = pl.multiple_of(t * 2 * blk, 2 * blk)
        out = []
        for hh in range(2):
            m, acc = st[2 * hh], st[2 * hh + 1]
            m_new = jnp.maximum(m, maxes[hh])
            for c in range(n_ck):
                rows_c = slice(c * ck, (c + 1) * ck)
                p_scr[hh, rows_c, :] = jnp.exp2(s2_scr[buf, hh, rows_c, :] - m_new).astype(BF16)
            acc = acc * jnp.exp2(m - m_new) + _dot(vaug_scr[hh, :, pl.ds(off, 2 * blk)], p_scr[hh])
            out += [m_new, acc]
        return out

    def far_pair(j, carry):
        st, max_a = list(carry[:4]), list(carry[4:])
        max_b = qk_tile(2 * j + 1, 1)
        st = soft_pv(2 * j, 0, max_a, st)
        max_a = qk_tile(jnp.minimum(2 * j + 2, n_it), 0)
        st = soft_pv(2 * j + 1, 1, max_b, st)
        return tuple(st + max_a)

    carry = lax.fori_loop(0, (n_it + 1) // 2, far_pair, tuple(state + qk_tile(0, 0)))
    state = carry[:4]
    for hh in range(2):
        acc = state[2 * hh + 1]
        o = acc[0:HEAD_DIM] / acc[HEAD_DIM:HEAD_DIM + 1]
        o_ref[0, :, hh * HEAD_DIM:(hh + 1) * HEAD_DIM] = o.T.astype(o_ref.dtype)


def _moba_prompt(q_t, k_t, v_t, bias_flat, bucket_near):
    nbat, _, t = q_t.shape
    nb = t // MOBA_BLOCK
    nbp = max(8, -(-nb // 8) * 8)
    assert t % MOBA_BLOCK == 0 and nbp + 3 <= HEAD_DIM
    kv = pl.BlockSpec((1, 2 * HEAD_DIM, t), lambda b, hp, qi, bias: (b, hp, 0))
    return pl.pallas_call(
        functools.partial(_moba_prompt_kernel, nb=nb, nbp=nbp),
        grid_spec=pltpu.PrefetchScalarGridSpec(
            num_scalar_prefetch=1,
            grid=(nbat, H_MOBA // 2, nb),
            in_specs=[pl.BlockSpec((1, 2 * HEAD_DIM, MOBA_BLOCK), lambda b, hp, qi, bias: (b, hp, qi)), kv, kv,
                      pl.BlockSpec((2 * MOBA_BLOCK, MOBA_BLOCK), lambda b, hp, qi, bias: (0, 0))],
            out_specs=pl.BlockSpec((1, MOBA_BLOCK, 2 * HEAD_DIM), lambda b, hp, qi, bias: (b, qi, hp)),
            scratch_shapes=[pltpu.VMEM((2, t + 3 * MOBA_BLOCK, 2 * HEAD_DIM), BF16),
                            pltpu.VMEM((2, V_ROWS, t + 3 * MOBA_BLOCK), BF16),
                            pltpu.VMEM((2, nbp, HEAD_DIM), F32),
                            pltpu.VMEM((2, 2 * MOBA_BLOCK, MOBA_BLOCK), F32),
                            pltpu.VMEM((2, 2 * MOBA_BLOCK, MOBA_BLOCK), F32),
                            pltpu.VMEM((2, 2 * MOBA_BLOCK, MOBA_BLOCK), BF16),
                            pltpu.VMEM((2, 2, 2 * MOBA_BLOCK, MOBA_BLOCK), F32)],
        ),
        out_shape=jax.ShapeDtypeStruct((nbat, t, MOBA_W), BF16),
        compiler_params=_cparams("parallel", "parallel", "arbitrary"),
        name="moba_prompt",
    )(bias_flat, q_t, k_t, v_t, bucket_near)


N_PAGE_BUF = 32


def _sample_select_kernel(pt_ref, q_ref, cache_ref, sel_ref, buf, sem, km_scr, *, layer, n_pages, pages_per_block, n_buf):
    b = pl.program_id(0)
    nblk = n_pages // pages_per_block
    page = cache_ref.shape[-1]

    def copy(p, slot):
        return pltpu.make_async_copy(cache_ref.at[layer, pt_ref[b, p]], buf.at[slot], sem.at[slot])

    for p in range(n_buf):
        copy(p, p).start()
    km_scr[...] = jnp.zeros(km_scr.shape, F32)

    def block(j, carry):
        tot = None
        for u in range(pages_per_block):
            p = j * pages_per_block + u
            slot = p % n_buf
            copy(p, slot).wait()
            x = buf[slot].reshape(MOBA_W, page)
            tot = x if tot is None else tot + x

            @pl.when(p + n_buf < n_pages)
            def _():
                copy(p + n_buf, slot).start()

        km_scr[pl.ds(j, 1), :] = jnp.sum(tot.T, axis=0, keepdims=True) * (1.0 / (pages_per_block * page))
        return carry

    lax.fori_loop(0, nblk, block, 0)

    km = km_scr[...]
    nrow = km.shape[0]
    head_sum = (lax.broadcasted_iota(jnp.int32, (MOBA_W, 128), 0) // HEAD_DIM
                == lax.broadcasted_iota(jnp.int32, (MOBA_W, 128), 1))
    head_sum = jnp.where(head_sum, 1.0, 0.0).astype(BF16)
    rown = lax.broadcasted_iota(jnp.int32, (nrow, 128), 0)
    rown_f = rown.astype(F32)
    sel_ref[0] = jnp.zeros(sel_ref.shape[1:], jnp.int32)
    for t in range(q_ref.shape[1]):
        gate = _rdot3(km * q_ref[0, t:t + 1, :], head_sum)
        g = jnp.where(rown < nblk, gate, -jnp.inf)
        for r in range(MOBA_TOPK):
            mx = jnp.max(g, axis=0, keepdims=True)
            idx = jnp.min(jnp.where(g == mx, rown_f, float(nrow)), axis=0, keepdims=True)
            sel_ref[0, 4 * t + r:4 * t + r + 1, :] = idx.astype(jnp.int32)
            g = jnp.where(rown_f == idx, -jnp.inf, g)


def _sample_select(page_table, q_tb, cache_t, layer):
    nbat, n_pages = page_table.shape
    n_t = q_tb.shape[1]
    page = cache_t.shape[-1]
    ppb = MOBA_BLOCK // page
    nblk = n_pages // ppb
    assert n_pages % ppb == 0
    n_buf = min(N_PAGE_BUF, n_pages)
    sel_rows = -(-4 * n_t // 8) * 8
    return pl.pallas_call(
        functools.partial(_sample_select_kernel, layer=layer, n_pages=n_pages, pages_per_block=ppb, n_buf=n_buf),
        grid_spec=pltpu.PrefetchScalarGridSpec(
            num_scalar_prefetch=1,
            grid=(nbat,),
            in_specs=[pl.BlockSpec((1, n_t, MOBA_W), lambda b, pt: (b, 0, 0)),
                      pl.BlockSpec(memory_space=pl.ANY)],
            out_specs=pl.BlockSpec((1, sel_rows, 128), lambda b, pt: (b, 0, 0)),
            scratch_shapes=[pltpu.VMEM((n_buf, H_MOBA, HEAD_DIM, page), F32),
                            pltpu.SemaphoreType.DMA((n_buf,)),
                            pltpu.VMEM((-(-nblk // 8) * 8, MOBA_W), F32)],
        ),
        out_shape=jax.ShapeDtypeStruct((nbat, sel_rows, 128), jnp.int32),
        compiler_params=_cparams("arbitrary"),
        name="sample_select",
    )(page_table, q_tb, cache_t)


def _sample_attend_kernel(pid_ref, sel_ref, q_ref, kn_ref, vn_ref, bias_ref, ck_ref, cv_ref, o_ref,
                          kbuf, vbuf, sem, *, layer, n_t, pages_per_block, nblk):
    b = pl.program_id(0)
    nbat = pl.num_programs(0)
    page = ck_ref.shape[-1]
    n_slot = n_t * MOBA_TOPK

    def copies(bb, h, par):
        out = []
        for slot in range(n_slot):
            for u in range(pages_per_block):
                pid = pid_ref[bb, (h * n_slot + slot) * pages_per_block + u]
                lanes = pl.ds(u * page, page)
                out.append(pltpu.make_async_copy(ck_ref.at[layer, pid, h], kbuf.at[par, slot, :, lanes], sem.at[par, 0]))
                out.append(pltpu.make_async_copy(cv_ref.at[layer, pid, h], vbuf.at[par, slot, :, lanes], sem.at[par, 1]))
        return out

    @pl.when(b == 0)
    def _():
        for cp in copies(b, 0, 0):
            cp.start()

    key_off = lax.broadcasted_iota(jnp.int32, (1, n_t), 1)
    for h in range(H_MOBA):
        par = h % 2
        if h + 1 < H_MOBA:
            for cp in copies(b, h + 1, 1 - par):
                cp.start()
        else:
            @pl.when(b + 1 < nbat)
            def _():
                for cp in copies(b + 1, 0, 1 - par):
                    cp.start()

        for cp in copies(b, h, par):
            cp.wait()
        rows = slice(h * HEAD_DIM, (h + 1) * HEAD_DIM)
        qh = q_ref[0, rows, :] * (HEAD_DIM ** -0.5)
        kn = kn_ref[0, rows, :]
        vn = vn_ref[0, rows, :]
        for t in range(n_t):
            qb = jnp.broadcast_to(qh[:, t:t + 1], (HEAD_DIM, MOBA_BLOCK))
            s_own = jnp.sum(qh[:, t:t + 1] * kn, axis=0, keepdims=True) + bias_ref[h, n_t + 1 + t:n_t + 2 + t, 0:n_t]
            s_own = jnp.where(key_off <= t, s_own, MASK_NEG)
            m = jnp.max(s_own, axis=1, keepdims=True)
            ss = []
            for r in range(MOBA_TOPK):
                slot = t * MOBA_TOPK + r
                j = sel_ref[b, h * n_slot + slot]
                prod = qb * kbuf[par, slot]
                s = jnp.sum(prod.reshape(HEAD_DIM // 8, 8, MOBA_BLOCK).sum(axis=0), axis=0, keepdims=True)
                s = s + jnp.where(j == nblk - 1, bias_ref[h, t:t + 1, :], bias_ref[h, n_t:n_t + 1, :])
                m = jnp.maximum(m, jnp.max(s, axis=1, keepdims=True))
                ss.append(s)
            p_own = jnp.exp(s_own - m)
            l = jnp.sum(p_own, axis=1, keepdims=True)
            wsum = None
            for r, s in enumerate(ss):
                p = jnp.exp(s - m)
                l = l + jnp.sum(p, axis=1, keepdims=True)
                term = p * vbuf[par, t * MOBA_TOPK + r]
                wsum = term if wsum is None else wsum + term
            acc = jnp.sum(wsum, axis=1, keepdims=True) + jnp.sum(p_own * vn, axis=1, keepdims=True)
            o_ref[0, rows, t:t + 1] = acc / l


def _sample_attend(pids, sel, q_bt, kn_bt, vn_bt, bias_rows, cache_kt, cache_vt, layer, nblk):
    nbat = pids.shape[0]
    n_t = q_bt.shape[-1]
    page = cache_kt.shape[-1]
    ppb = MOBA_BLOCK // page
    n_slot = n_t * MOBA_TOPK
    new = pl.BlockSpec((1, MOBA_W, n_t), lambda b, pd, sl: (b, 0, 0))
    return pl.pallas_call(
        functools.partial(_sample_attend_kernel, layer=layer, n_t=n_t, pages_per_block=ppb, nblk=nblk),
        grid_spec=pltpu.PrefetchScalarGridSpec(
            num_scalar_prefetch=2,
            grid=(nbat,),
            in_specs=[new, new, new,
                      pl.BlockSpec(bias_rows.shape, lambda b, pd, sl: (0, 0, 0)),
                      pl.BlockSpec(memory_space=pl.ANY), pl.BlockSpec(memory_space=pl.ANY)],
            out_specs=new,
            scratch_shapes=[pltpu.VMEM((2, n_slot, HEAD_DIM, MOBA_BLOCK), F32),
                            pltpu.VMEM((2, n_slot, HEAD_DIM, MOBA_BLOCK), F32),
                            pltpu.SemaphoreType.DMA((2, 2))],
        ),
        out_shape=jax.ShapeDtypeStruct((nbat, MOBA_W, n_t), F32),
        compiler_params=_cparams("arbitrary"),
        name="sample_attend",
    )(pids, sel, q_bt, kn_bt, vn_bt, bias_rows, cache_kt, cache_vt)


def _outproj_kernel(x_ref, olin_ref, omoba_ref, wl_ref, wm_ref, g_ref, y_ref):
    mix = _dot(olin_ref[...], wl_ref[...]) + _dot(omoba_ref[...], wm_ref[...])
    y_ref[...] = x_ref[...] + _rms(mix, g_ref[...])


def _outproj(x, olin, omoba, w_lin, w_moba, g, tm):
    n, d = x.shape
    full = lambda s: pl.BlockSpec(s, lambda i: (0,) * len(s))
    rows = lambda w: pl.BlockSpec((tm, w), lambda i: (i, 0))
    return pl.pallas_call(
        _outproj_kernel,
        grid=(n // tm,),
        in_specs=[rows(d), rows(olin.shape[1]), rows(omoba.shape[1]), full(w_lin.shape), full(w_moba.shape), full((1, d))],
        out_specs=rows(d),
        out_shape=jax.ShapeDtypeStruct((n, d), F32),
        compiler_params=_cparams("parallel"),
        name="outproj",
    )(x, olin, omoba, w_lin, w_moba, g)


def _ffn_kernel(x_ref, gpre_ref, wg_ref, wu_ref, wd_ref, gpost_ref, y_ref, h_scr, acc_scr):
    j = pl.program_id(1)

    @pl.when(j == 0)
    def _():
        h_scr[...] = _rms(x_ref[...], gpre_ref[...]).astype(BF16)
        acc_scr[...] = jnp.zeros(acc_scr.shape, F32)

    h = h_scr[...]
    gate = _dot(h, wg_ref[...])
    up = _dot(h, wu_ref[...])
    act = (gate / (1.0 + jnp.exp(-gate)) * up).astype(BF16)
    acc_scr[...] += _dot(act, wd_ref[...])

    @pl.when(j == pl.num_programs(1) - 1)
    def _():
        y_ref[...] = x_ref[...] + _rms(acc_scr[...], gpost_ref[...])


def _ffn(x, gpre, wg, wu, wd, gpost, tm, fc):
    n, d = x.shape
    dff = wg.shape[1]
    rows = pl.BlockSpec((tm, d), lambda i, j: (i, 0))
    vec = pl.BlockSpec((1, d), lambda i, j: (0, 0))
    return pl.pallas_call(
        _ffn_kernel,
        grid=(n // tm, dff // fc),
        in_specs=[rows, vec, pl.BlockSpec((d, fc), lambda i, j: (0, j)), pl.BlockSpec((d, fc), lambda i, j: (0, j)),
                  pl.BlockSpec((fc, d), lambda i, j: (j, 0)), vec],
        out_specs=rows,
        out_shape=jax.ShapeDtypeStruct((n, d), F32),
        scratch_shapes=[pltpu.VMEM((tm, d), BF16), pltpu.VMEM((tm, d), F32)],
        compiler_params=_cparams("parallel", "arbitrary"),
        name="ffn",
    )(x, gpre, wg, wu, wd, gpost)


def _t5_bucket_np(dist):
    n = np.maximum(dist, 0)
    max_exact = N_BUCKETS // 2
    nf = np.maximum(n, 1).astype(np.float32)
    large = max_exact + (np.log(nf / np.float32(max_exact)) / np.float32(math.log(MAX_DISTANCE / max_exact))
                         * np.float32(N_BUCKETS - max_exact)).astype(np.int32)
    large = np.minimum(large, N_BUCKETS - 1)
    return np.where(n < max_exact, n, large).astype(np.int32)


def _rotary_tables(pos):
    half = LIN_KEY_DIM // 2
    inv = 1.0 / (ROPE_BASE ** jnp.linspace(0.0, 1.0, half, dtype=F32))
    ang = pos.astype(F32)[:, None] * inv[None, :]
    tile = lambda a: jnp.tile(a, (1, 2 * N_LIN_HEADS))
    return tile(jnp.cos(ang)), tile(jnp.sin(ang))


def _blockdiag_state(s):
    eye = jnp.eye(N_LIN_HEADS, dtype=s.dtype)
    return (s[:, :, :, None, :] * eye[None, :, None, :, None]).reshape(s.shape[0], LIN_QK_W, LIN_V_W)


def _head_states(sb):
    x = sb.reshape(sb.shape[0], N_LIN_HEADS, LIN_KEY_DIM, N_LIN_HEADS, HEAD_DIM)
    return jnp.stack([x[:, h, :, h, :] for h in range(N_LIN_HEADS)], axis=1)


def _row_tile(n, want):
    t = min(n, want)
    while n % t:
        t //= 2
    return t


def kernel(x_prompt, x_sample, cache_k, cache_v, state_ret, state_gla, page_table, w_in, w_out, gla_gate_w2, gla_gate_b, ret_norm_g, gla_norm_g, norm_attn_pre, norm_attn_post, norm_ffn_pre, norm_ffn_post, w_ffn_gate, w_ffn_up, w_ffn_down, rel_bias):
    depth = w_in.shape[0]
    nbp_, t_p, d_model = x_prompt.shape
    nbs, t_s, _ = x_sample.shape
    n_pages = page_table.shape[1]
    page = cache_k.shape[2]
    past_len = n_pages * page
    assert past_len % MOBA_BLOCK == 0 and t_s <= MOBA_BLOCK and t_p % MOBA_BLOCK == 0
    dff = w_ffn_gate.shape[2]

    cos_p, sin_p = _rotary_tables(jnp.arange(t_p, dtype=jnp.int32))
    cos_s, sin_s = _rotary_tables(past_len + jnp.arange(t_s, dtype=jnp.int32))
    off = np.arange(MOBA_BLOCK)
    dist_own = off[None, :] - off[:, None]
    bucket_near = np.concatenate([_t5_bucket_np(MOBA_BLOCK + dist_own),
                                  np.where(dist_own >= 0, _t5_bucket_np(dist_own), -1)], axis=0).astype(np.int32)
    bias_h = rel_bias.astype(F32).T
    bias_flat = bias_h.reshape(-1)
    far = bias_h[:, N_BUCKETS - 1]
    prev_rows = bias_h[:, _t5_bucket_np(MOBA_BLOCK + np.arange(t_s)[:, None] - off[None, :])]
    far_row = jnp.broadcast_to(far[:, None, None], (H_MOBA, 1, MOBA_BLOCK))
    own_rows = bias_h[:, _t5_bucket_np(np.arange(t_s)[:, None] - off[None, :])]
    n_rows = 2 * t_s + 1
    pad_rows = -(-n_rows // 8) * 8 - n_rows
    bias_rows = jnp.concatenate([prev_rows, far_row, own_rows, jnp.zeros((H_MOBA, pad_rows, MOBA_BLOCK), F32)], axis=1)

    rdec = jnp.repeat(jnp.log(1.0 - 2.0 ** (-5.0 - jnp.arange(N_LIN_HEADS, dtype=F32))), LIN_KEY_DIM)[None, :]
    cache_kt = jnp.transpose(cache_k, (0, 1, 3, 4, 2))
    cache_vt = jnp.transpose(cache_v, (0, 1, 3, 4, 2))
    w_in_t = jnp.swapaxes(w_in, 1, 2)

    hp = x_prompt
    hs = x_sample.reshape(1, nbs * t_s, d_model)
    zero_state = jnp.zeros((nbp_, 2, LIN_QK_W, LIN_V_W), F32)
    tm_p = _row_tile(t_p, 512)
    tb_p = _row_tile(t_p, 128)
    c_p = math.gcd(t_p, REC_CHUNK)
    c_s = math.gcd(t_s, REC_CHUNK)
    fc = dff // 2 if (dff // 2) % 128 == 0 else dff
    outs = {k: [] for k in ("kp", "vp", "rp", "gp", "ks", "vs", "rs", "gs")}

    for l in range(depth):
        seg = w_in_t[l]
        wtok = jnp.concatenate([seg[:LIN_TOK_W + GLA_RANK], jnp.zeros((GA_PAD - GLA_RANK, d_model), F32)], axis=0).T.astype(BF16)
        m0 = LIN_TOK_W + GLA_RANK
        wq_t = seg[m0:m0 + MOBA_W].astype(BF16)
        wk_t = seg[m0 + MOBA_W:m0 + 2 * MOBA_W].astype(BF16)
        wv_t = seg[m0 + 2 * MOBA_W:m0 + 3 * MOBA_W].astype(BF16)
        w2p = jnp.concatenate([gla_gate_w2[l], jnp.zeros((GA_PAD - GLA_RANK, LIN_QK_W), F32)], axis=0).astype(BF16)
        gb = gla_gate_b[l][None, :]
        rgain = jnp.tile(ret_norm_g[l], N_LIN_HEADS)[None, :]
        ggain = jnp.tile(gla_norm_g[l], N_LIN_HEADS)[None, :]
        w_o = w_out[l].astype(BF16)
        w_o_lin, w_o_moba = w_o[:2 * LIN_V_W], w_o[2 * LIN_V_W:]
        wg, wu, wd = w_ffn_gate[l].astype(BF16), w_ffn_up[l].astype(BF16), w_ffn_down[l].astype(BF16)
        g_pre, g_post = norm_attn_pre[l][None, :], norm_attn_post[l][None, :]
        f_pre, f_post = norm_ffn_pre[l][None, :], norm_ffn_post[l][None, :]

        tok, q_t, k_t, v_t = _inproj(hp, g_pre, wtok, wq_t, wk_t, wv_t, tm_p)
        olin, sfin = _linrec(tok, cos_p, sin_p, w2p, gb, rdec, rgain, ggain, zero_state, tb_p, c_p)
        omoba = _moba_prompt(q_t, k_t, v_t, bias_flat, bucket_near)
        x1 = _outproj(hp.reshape(-1, d_model), olin.reshape(-1, 2 * LIN_V_W), omoba.reshape(-1, MOBA_W),
                      w_o_lin, w_o_moba, g_post, tm_p)
        hp = _ffn(x1, f_pre, wg, wu, wd, f_post, tm_p, fc).reshape(nbp_, t_p, d_model)
        outs["kp"].append(jnp.transpose(k_t.reshape(nbp_, H_MOBA, HEAD_DIM, t_p), (0, 3, 1, 2)))
        outs["vp"].append(jnp.transpose(v_t.reshape(nbp_, H_MOBA, HEAD_DIM, t_p), (0, 3, 1, 2)))
        outs["rp"].append(_head_states(sfin[:, 0]))
        outs["gp"].append(_head_states(sfin[:, 1]))

        n_s = nbs * t_s
        tok, q_t, k_t, v_t = _inproj(hs, g_pre, wtok, wq_t, wk_t, wv_t, n_s)
        s0 = jnp.stack([_blockdiag_state(state_ret[l]), _blockdiag_state(state_gla[l])], axis=1)
        olin, sfin = _linrec(tok.reshape(nbs, t_s, TOK_W), cos_s, sin_s, w2p, gb, rdec, rgain, ggain, s0, t_s, c_s)
        per_batch = lambda a: jnp.transpose(a.reshape(MOBA_W, nbs, t_s), (1, 0, 2))
        q_bt, k_bt, v_bt = per_batch(q_t), per_batch(k_t), per_batch(v_t)
        sel = _sample_select(page_table, jnp.transpose(q_bt, (0, 2, 1)), cache_kt, l)
        sel = jnp.transpose(sel[:, :4 * t_s, :H_MOBA].reshape(nbs, t_s, 4, H_MOBA)[:, :, :MOBA_TOPK], (0, 3, 1, 2))
        sel = sel.reshape(nbs, -1)
        ppb = MOBA_BLOCK // page
        logical = (sel[:, :, None] * ppb + jnp.arange(ppb, dtype=jnp.int32)[None, None, :]).reshape(nbs, -1)
        hit = logical[:, :, None] == jnp.arange(n_pages, dtype=jnp.int32)[None, None, :]
        pids = jnp.sum(jnp.where(hit, page_table[:, None, :], 0), axis=-1)
        o_bt = _sample_attend(pids, sel, q_bt, k_bt, v_bt, bias_rows, cache_kt, cache_vt, l, n_pages // ppb)
        omoba = jnp.transpose(o_bt, (0, 2, 1)).reshape(n_s, MOBA_W).astype(BF16)
        x1 = _outproj(hs.reshape(n_s, d_model), olin.reshape(n_s, 2 * LIN_V_W), omoba, w_o_lin, w_o_moba, g_post, n_s)
        hs = _ffn(x1, f_pre, wg, wu, wd, f_post, n_s, fc).reshape(1, n_s, d_model)
        to_cache = lambda a: jnp.transpose(a.reshape(H_MOBA, HEAD_DIM, nbs, t_s), (2, 3, 0, 1))
        outs["ks"].append(to_cache(k_t))
        outs["vs"].append(to_cache(v_t))
        outs["rs"].append(_head_states(sfin[:, 0]))
        outs["gs"].append(_head_states(sfin[:, 1]))

    st = lambda k: jnp.stack(outs[k])
    return (hp, hs.reshape(nbs, t_s, d_model), st("kp"), st("vp"), st("rp"), st("gp"),
            st("ks"), st("vs"), st("rs"), st("gs"))
```

```python
import functools
import math

import jax
import jax.numpy as jnp
import numpy as np
from jax import lax
from jax.experimental import pallas as pl
from jax.experimental.pallas import tpu as pltpu

F32 = jnp.float32
BF16 = jnp.bfloat16

HEAD_DIM = 64
LIN_KEY_DIM = 32
N_LIN_HEADS = 4
H_MOBA = 8
LIN_QK_W = N_LIN_HEADS * LIN_KEY_DIM
LIN_V_W = N_LIN_HEADS * HEAD_DIM
MOBA_W = H_MOBA * HEAD_DIM
GLA_RANK = 16
GLA_TAU = 16.0
REC_CHUNK = 16
ROPE_BASE = 10000.0
MOBA_BLOCK = 256
MOBA_TOPK = 3
N_BUCKETS = 32
MAX_DISTANCE = 128
RMS_EPS = 1e-6
LIN_TOK_W = 2 * (2 * LIN_QK_W + 2 * LIN_V_W)
GA_PAD = 128
TOK_W = LIN_TOK_W + GA_PAD

VMEM_LIMIT_BYTES = 56 * 1024 * 1024
MASK_NEG = -1e30
LOG2E = 1.4426950408889634
V_ROWS = 80

_NT = (((1,), (1,)), ((), ()))
_TN = (((0,), (0,)), ((), ()))


def _cparams(*sem):
    return pltpu.CompilerParams(dimension_semantics=sem, vmem_limit_bytes=VMEM_LIMIT_BYTES)


def _rms(x, g):
    return x * lax.rsqrt(jnp.mean(x * x, axis=-1, keepdims=True) + RMS_EPS) * g


def _split3(x):
    hi = x.astype(BF16)
    r = x - hi.astype(F32)
    mid = r.astype(BF16)
    lo = (r - mid.astype(F32)).astype(BF16)
    return hi, mid, lo


def _dot(a, b):
    return jnp.dot(a, b, preferred_element_type=F32)


def _ldot3(a_bf16, x):
    hi, mid, lo = _split3(x)
    return _dot(a_bf16, hi) + _dot(a_bf16, mid) + _dot(a_bf16, lo)


def _rdot3(x, b_bf16):
    hi, mid, lo = _split3(x)
    return _dot(hi, b_bf16) + _dot(mid, b_bf16) + _dot(lo, b_bf16)


def _inproj_kernel(x_ref, g_ref, wtok_ref, wq_ref, wk_ref, wv_ref, tok_ref, q_ref, k_ref, v_ref):
    hb = _rms(x_ref[0], g_ref[...]).astype(BF16)
    tok_ref[0] = _dot(hb, wtok_ref[...])
    q_ref[0] = lax.dot_general(wq_ref[...], hb, _NT, preferred_element_type=F32)
    k_ref[0] = lax.dot_general(wk_ref[...], hb, _NT, preferred_element_type=F32)
    v_ref[0] = lax.dot_general(wv_ref[...], hb, _NT, preferred_element_type=F32)


def _inproj(x, g, wtok, wq_t, wk_t, wv_t, tm):
    nb, t, d = x.shape
    full = lambda s: pl.BlockSpec(s, lambda b, i: (0,) * len(s))
    chn = pl.BlockSpec((1, MOBA_W, tm), lambda b, i: (b, 0, i))
    return pl.pallas_call(
        _inproj_kernel,
        grid=(nb, t // tm),
        in_specs=[pl.BlockSpec((1, tm, d), lambda b, i: (b, i, 0)), full((1, d)), full((d, TOK_W)),
                  full((MOBA_W, d)), full((MOBA_W, d)), full((MOBA_W, d))],
        out_specs=[pl.BlockSpec((1, tm, TOK_W), lambda b, i: (b, i, 0)), chn, chn, chn],
        out_shape=[jax.ShapeDtypeStruct((nb, t, TOK_W), F32)] + [jax.ShapeDtypeStruct((nb, MOBA_W, t), F32)] * 3,
        compiler_params=_cparams("parallel", "parallel"),
        name="inproj",
    )(x, g, wtok, wq_t, wk_t, wv_t)


def _linrec_kernel(tok_ref, cos_ref, sin_ref, w2_ref, gb_ref, rdec_ref, rgain_ref, ggain_ref, s0_ref,
                   o_ref, sfin_ref, s_scr, *, tb, c, bpb):
    t = pl.program_id(1)

    @pl.when(t == 0)
    def _():
        s_scr[...] = s0_ref[...]

    n_sub = tb // c
    shift = int(math.log2(c))
    row = lax.broadcasted_iota(jnp.int32, (tb, tb), 0)
    col = lax.broadcasted_iota(jnp.int32, (tb, tb), 1)
    same = (row >> shift) == (col >> shift)
    tril = same & (col <= row)
    tril_b = jnp.where(tril, 1.0, 0.0).astype(BF16)
    same_b = jnp.where(same, 1.0, 0.0).astype(BF16)
    head_qk = lax.broadcasted_iota(jnp.int32, (1, LIN_QK_W), 1) // LIN_KEY_DIM
    head_v = lax.broadcasted_iota(jnp.int32, (1, LIN_V_W), 1) // HEAD_DIM
    blockdiag = (lax.broadcasted_iota(jnp.int32, (LIN_QK_W, LIN_V_W), 0) // LIN_KEY_DIM
                 == lax.broadcasted_iota(jnp.int32, (LIN_QK_W, LIN_V_W), 1) // HEAD_DIM)
    gi = lax.broadcasted_iota(jnp.int32, (LIN_V_W, LIN_V_W), 0) // HEAD_DIM
    gj = lax.broadcasted_iota(jnp.int32, (LIN_V_W, LIN_V_W), 1) // HEAD_DIM
    head_mean = jnp.where(gi == gj, 1.0 / HEAD_DIM, 0.0).astype(BF16)
    first_half = (lax.broadcasted_iota(jnp.int32, (1, LIN_QK_W), 1) % LIN_KEY_DIM) < LIN_KEY_DIM // 2
    cos = cos_ref[...]
    sin = sin_ref[...]
    qk_scale = LIN_KEY_DIM ** -0.5

    def rotary(x):
        rot = jnp.where(first_half, -pltpu.roll(x, LIN_QK_W - LIN_KEY_DIM // 2, 1), pltpu.roll(x, LIN_KEY_DIM // 2, 1))
        return x * cos + rot * sin

    sub_of_col = lax.broadcasted_iota(jnp.int32, (1, tb), 1) >> shift

    def group(bi, g_idx, q, k, v, b, b_end, gate, gain):
        q_dec = q * jnp.exp(b)
        k_inc = (k * jnp.exp(-b)).astype(BF16)
        k_dec = k * jnp.exp(b_end - b)
        b_end_t = b_end.T
        k_dec_t = k_dec.T
        vb = v.astype(BF16)
        q4 = jnp.concatenate([jnp.where(head_qk == h, q_dec, 0.0) for h in range(N_LIN_HEADS)], axis=0).astype(BF16)
        sc = lax.dot_general(q4, k_inc, _NT, preferred_element_type=F32)
        p4 = jnp.concatenate([jnp.where(tril, sc[h * tb:(h + 1) * tb], 0.0) for h in range(N_LIN_HEADS)], axis=0).astype(BF16)
        pv = _dot(p4, vb)
        o = jnp.where(head_v == 0, pv[0:tb], 0.0)
        for h in range(1, N_LIN_HEADS):
            o = o + jnp.where(head_v == h, pv[h * tb:(h + 1) * tb], 0.0)
        if n_sub == 1:
            u_all = lax.dot_general(k_dec.astype(BF16), vb, _TN, preferred_element_type=F32)
        else:
            k_stack = jnp.concatenate([jnp.where(sub_of_col == i, k_dec_t, 0.0) for i in range(n_sub)], axis=0)
            u_all = _dot(k_stack.astype(BF16), vb)
        s = s_scr[bi, g_idx]
        inter = []
        for i in range(n_sub):
            lo, hi = i * c, (i + 1) * c
            inter.append(_dot(q_dec[lo:hi].astype(BF16), s.astype(BF16)))
            u = u_all[i * LIN_QK_W:(i + 1) * LIN_QK_W]
            s = jnp.exp(b_end_t[:, lo:lo + 1]) * s + jnp.where(blockdiag, u, 0.0)
        s_scr[bi, g_idx] = s
        o = o + (inter[0] if n_sub == 1 else jnp.concatenate(inter, axis=0))
        ms = _rdot3(o * o, head_mean)
        on = o * lax.rsqrt(ms + RMS_EPS) * gain
        return on * (gate / (1.0 + jnp.exp(-gate)))

    a = LIN_QK_W
    w = LIN_V_W
    base = 2 * a + 2 * w
    for bi in range(bpb):
        rq = rotary(tok_ref[bi, :, 0:a])
        rk = rotary(tok_ref[bi, :, a:2 * a]) * qk_scale
        rv = tok_ref[bi, :, 2 * a:2 * a + w]
        rgate = tok_ref[bi, :, 2 * a + w:2 * a + 2 * w]
        pos_in_chunk = (lax.broadcasted_iota(jnp.int32, (tb, 1), 0) & (c - 1)).astype(F32)
        ret_b = (pos_in_chunk + 1.0) * rdec_ref[...]
        ret_b_end = jnp.broadcast_to(float(c) * rdec_ref[...], (tb, LIN_QK_W))
        o_ref[bi, :, 0:w] = group(bi, 0, rq, rk, rv, ret_b, ret_b_end, rgate, rgain_ref[...]).astype(o_ref.dtype)

        gq = tok_ref[bi, :, base:base + a] * qk_scale
        gk = tok_ref[bi, :, base + a:base + 2 * a]
        gv = tok_ref[bi, :, base + 2 * a:base + 2 * a + w]
        ggate = tok_ref[bi, :, base + 2 * a + w:base + 2 * a + 2 * w]
        ga = tok_ref[bi, :, LIN_TOK_W:TOK_W]
        z = _dot(ga.astype(BF16), w2_ref[...]) + gb_ref[...]
        gla_lg = (jnp.minimum(z, 0.0) - jnp.log(1.0 + jnp.exp(-jnp.abs(z)))) * (1.0 / GLA_TAU)
        gla_b = _ldot3(tril_b, gla_lg)
        gla_b_end = _ldot3(same_b, gla_lg)
        o_ref[bi, :, w:2 * w] = group(bi, 1, gq, gk, gv, gla_b, gla_b_end, ggate, ggain_ref[...]).astype(o_ref.dtype)

    @pl.when(t == pl.num_programs(1) - 1)
    def _():
        sfin_ref[...] = s_scr[...]


def _linrec(tok, cos, sin, w2p, gb, rdec, rgain, ggain, s0, tb, c):
    nb, t, _ = tok.shape
    bpb = 2 if nb % 2 == 0 else 1
    full = lambda s: pl.BlockSpec(s, lambda b, i: (0,) * len(s))
    st = pl.BlockSpec((bpb, 2, LIN_QK_W, LIN_V_W), lambda b, i: (b, 0, 0, 0))
    return pl.pallas_call(
        functools.partial(_linrec_kernel, tb=tb, c=c, bpb=bpb),
        grid=(nb // bpb, t // tb),
        in_specs=[pl.BlockSpec((bpb, tb, TOK_W), lambda b, i: (b, i, 0)),
                  pl.BlockSpec((tb, LIN_QK_W), lambda b, i: (i, 0)),
                  pl.BlockSpec((tb, LIN_QK_W), lambda b, i: (i, 0)),
                  full((GA_PAD, LIN_QK_W)), full((1, LIN_QK_W)), full((1, LIN_QK_W)),
                  full((1, LIN_V_W)), full((1, LIN_V_W)), st],
        out_specs=[pl.BlockSpec((bpb, tb, 2 * LIN_V_W), lambda b, i: (b, i, 0)), st],
        out_shape=[jax.ShapeDtypeStruct((nb, t, 2 * LIN_V_W), BF16),
                   jax.ShapeDtypeStruct((nb, 2, LIN_QK_W, LIN_V_W), F32)],
        scratch_shapes=[pltpu.VMEM((bpb, 2, LIN_QK_W, LIN_V_W), F32)],
        compiler_params=_cparams("parallel", "arbitrary"),
        name="linrec",
    )(tok, cos, sin, w2p, gb, rdec, rgain, ggain, s0)


def _select_topk(gate_t, n_valid, nbp, width):
    rown = lax.broadcasted_iota(jnp.int32, (nbp, width), 0)
    rown_f = rown.astype(F32)
    valid = rown < n_valid
    g = jnp.where(valid, gate_t, -jnp.inf)
    sel = jnp.zeros((nbp, width), F32)
    for _ in range(MOBA_TOPK):
        mx = jnp.max(g, axis=0, keepdims=True)
        idx = jnp.min(jnp.where(g == mx, rown_f, float(nbp)), axis=0, keepdims=True)
        pick = rown_f == idx
        sel = jnp.where(pick, jnp.where(valid, 1.0, 0.0), sel)
        g = jnp.where(pick, -jnp.inf, g)
    return sel, rown


def _moba_prompt_kernel(bias_ref, q_ref, k_ref, v_ref, bucket_ref, o_ref,
                        kaug_scr, vaug_scr, kmean_scr, near_scr, s_scr, p_scr, s2_scr, *, nb, nbp):
    hp = pl.program_id(1)
    qi = pl.program_id(2)
    blk = MOBA_BLOCK
    n_tail = HEAD_DIM - nbp
    c_far, c_dummy = nbp, nbp + 2

    @pl.when(qi == 0)
    def _():
        lane = lax.broadcasted_iota(jnp.int32, (blk, HEAD_DIM), 1)
        lane_aug = lax.broadcasted_iota(jnp.int32, (blk, 2 * HEAD_DIM), 1)
        ones_row = jnp.where(lax.broadcasted_iota(jnp.int32, (V_ROWS - HEAD_DIM, blk), 0) == 0, 1.0, 0.0).astype(BF16)
        for hh in range(2):
            head = hp * 2 + hh
            rows = slice(hh * HEAD_DIM, (hh + 1) * HEAD_DIM)
            for d0 in (0, (nb + 1) * blk, (nb + 2) * blk):
                kaug_scr[hh, d0:d0 + blk, :] = jnp.where(lane_aug == HEAD_DIM + c_dummy, 1.0, 0.0).astype(BF16)
                vaug_scr[hh, 0:HEAD_DIM, d0:d0 + blk] = jnp.zeros((HEAD_DIM, blk), BF16)
                vaug_scr[hh, HEAD_DIM:V_ROWS, d0:d0 + blk] = ones_row
            kmean_scr[hh] = jnp.zeros((nbp, HEAD_DIM), F32)

            def fill(n, carry):
                off = pl.multiple_of(n * blk, blk)
                dst = pl.ds(pl.multiple_of((n + 1) * blk, blk), blk)
                kb = k_ref[0, rows, pl.ds(off, blk)].T
                kmean_scr[hh, pl.ds(n, 1), :] = jnp.mean(kb, axis=0, keepdims=True)
                extra = jnp.where((lane == n) | (lane == c_far) | (lane == c_far + 1), 1.0, 0.0)
                kaug_scr[hh, dst, :] = jnp.concatenate([kb, extra], axis=1).astype(BF16)
                vaug_scr[hh, 0:HEAD_DIM, dst] = v_ref[0, rows, pl.ds(off, blk)].astype(BF16)
                vaug_scr[hh, HEAD_DIM:V_ROWS, dst] = ones_row
                return carry

            lax.fori_loop(0, nb, fill, 0)

            far_b = bias_ref[head * N_BUCKETS + N_BUCKETS - 1]
            for c0 in range(0, 2 * blk, 128):
                bk = bucket_ref[c0:c0 + 128, :]

                def pick(bi, tab):
                    return jnp.where(bk == bi, (bias_ref[head * N_BUCKETS + bi] - far_b) * LOG2E, tab)

                near_scr[hh, c0:c0 + 128, :] = lax.fori_loop(0, N_BUCKETS, pick, jnp.where(bk < 0, MASK_NEG, 0.0))

    tail_row = lax.broadcasted_iota(jnp.int32, (n_tail, blk), 0)
    near_off = pl.multiple_of(qi * blk, blk)
    state = []
    q_far = []
    for hh in range(2):
        head = hp * 2 + hh
        q_t = q_ref[0, hh * HEAD_DIM:(hh + 1) * HEAD_DIM, :]
        gate_t = jnp.dot(kmean_scr[hh], q_t, preferred_element_type=F32, precision=lax.Precision.HIGHEST)
        sel, rown = _select_topk(gate_t, qi, nbp, blk)
        near_ok = (rown == qi) | ((rown == qi - 1) & (sel > 0.0))
        far_ok = (rown < qi - 1) & (sel > 0.0)
        far_v = jnp.full((n_tail, blk), bias_ref[head * N_BUCKETS + N_BUCKETS - 1] * LOG2E, F32)
        far_hi = far_v.astype(BF16).astype(F32)
        tail = jnp.where(tail_row == 0, far_hi, jnp.where(tail_row == 1, far_v - far_hi,
                                                          jnp.where(tail_row == 2, MASK_NEG, 0.0)))
        qs = q_t * (HEAD_DIM ** -0.5 * LOG2E)
        q_near = jnp.concatenate([qs, jnp.where(near_ok, 0.0, MASK_NEG), tail], axis=0).astype(BF16)
        q_far.append(jnp.concatenate([qs, jnp.where(far_ok, 0.0, MASK_NEG), tail], axis=0).astype(BF16))
        s_scr[hh] = _dot(kaug_scr[hh, pl.ds(near_off, 2 * blk), :], q_near) + near_scr[hh]
    for hh in range(2):
        m = jnp.max(s_scr[hh], axis=0, keepdims=True)
        p_scr[hh] = jnp.exp2(s_scr[hh] - m).astype(BF16)
        state += [m, _dot(vaug_scr[hh, :, pl.ds(near_off, 2 * blk)], p_scr[hh])]

    ck = 256
    n_ck = 2 * blk // ck
    n_it = (qi + 1) // 2

    def qk_tile(t, buf):
        off = pl.multiple_of(t * 2 * blk, 2 * blk)
        maxes = []
        for hh in range(2):
            mloc = None
            for c in range(n_ck):
                sc = _dot(kaug_scr[hh, pl.ds(off + c * ck, ck), :], q_far[hh])
                s2_scr[buf, hh, c * ck:(c + 1) * ck, :] = sc
                cm = jnp.max(sc.reshape(ck // 8, 8, blk), axis=0)
                mloc = cm if mloc is None else jnp.maximum(mloc, cm)
            maxes.append(jnp.max(mloc, axis=0, keepdims=True))
        return maxes

    def soft_pv(t, buf, maxes, st):
        off = pl.multiple_of(t * 2 * blk, 2 * blk)
        out = []
        for hh in range(2):
            m, acc = st[2 * hh], st[2 * hh + 1]
            m_new = jnp.maximum(m, maxes[hh])
            for c in range(n_ck):
                rows_c = slice(c * ck, (c + 1) * ck)
                p_scr[hh, rows_c, :] = jnp.exp2(s2_scr[buf, hh, rows_c, :] - m_new).astype(BF16)
            acc = acc * jnp.exp2(m - m_new) + _dot(vaug_scr[hh, :, pl.ds(off, 2 * blk)], p_scr[hh])
            out += [m_new, acc]
        return out

    def far_pair(j, carry):
        st, max_a = list(carry[:4]), list(carry[4:])
        max_b = qk_tile(2 * j + 1, 1)
        st = soft_pv(2 * j, 0, max_a, st)
        max_a = qk_tile(jnp.minimum(2 * j + 2, n_it), 0)
        st = soft_pv(2 * j + 1, 1, max_b, st)
        return tuple(st + max_a)

    carry = lax.fori_loop(0, (n_it + 1) // 2, far_pair, tuple(state + qk_tile(0, 0)))
    state = carry[:4]
    for hh in range(2):
        acc = state[2 * hh + 1]
        o = acc[0:HEAD_DIM] / acc[HEAD_DIM:HEAD_DIM + 1]
        o_ref[0, :, hh * HEAD_DIM:(hh + 1) * HEAD_DIM] = o.T.astype(o_ref.dtype)


def _moba_prompt(q_t, k_t, v_t, bias_flat, bucket_near):
    nbat, _, t = q_t.shape
    nb = t // MOBA_BLOCK
    nbp = max(8, -(-nb // 8) * 8)
    assert t % MOBA_BLOCK == 0 and nbp + 3 <= HEAD_DIM
    kv = pl.BlockSpec((1, 2 * HEAD_DIM, t), lambda b, hp, qi, bias: (b, hp, 0))
    return pl.pallas_call(
        functools.partial(_moba_prompt_kernel, nb=nb, nbp=nbp),
        grid_spec=pltpu.PrefetchScalarGridSpec(
            num_scalar_prefetch=1,
            grid=(nbat, H_MOBA // 2, nb),
            in_specs=[pl.BlockSpec((1, 2 * HEAD_DIM, MOBA_BLOCK), lambda b, hp, qi, bias: (b, hp, qi)), kv, kv,
                      pl.BlockSpec((2 * MOBA_BLOCK, MOBA_BLOCK), lambda b, hp, qi, bias: (0, 0))],
            out_specs=pl.BlockSpec((1, MOBA_BLOCK, 2 * HEAD_DIM), lambda b, hp, qi, bias: (b, qi, hp)),
            scratch_shapes=[pltpu.VMEM((2, t + 3 * MOBA_BLOCK, 2 * HEAD_DIM), BF16),
                            pltpu.VMEM((2, V_ROWS, t + 3 * MOBA_BLOCK), BF16),
                            pltpu.VMEM((2, nbp, HEAD_DIM), F32),
                            pltpu.VMEM((2, 2 * MOBA_BLOCK, MOBA_BLOCK), F32),
                            pltpu.VMEM((2, 2 * MOBA_BLOCK, MOBA_BLOCK), F32),
                            pltpu.VMEM((2, 2 * MOBA_BLOCK, MOBA_BLOCK), BF16),
                            pltpu.VMEM((2, 2, 2 * MOBA_BLOCK, MOBA_BLOCK), F32)],
        ),
        out_shape=jax.ShapeDtypeStruct((nbat, t, MOBA_W), BF16),
        compiler_params=_cparams("parallel", "parallel", "arbitrary"),
        name="moba_prompt",
    )(bias_flat, q_t, k_t, v_t, bucket_near)


N_PAGE_BUF = 32


def _sample_select_kernel(pt_ref, q_ref, cache_ref, sel_ref, buf, sem, km_scr, *, layer, n_pages, pages_per_block, n_buf):
    b = pl.program_id(0)
    nblk = n_pages // pages_per_block
    page = cache_ref.shape[-1]

    def copy(p, slot):
        return pltpu.make_async_copy(cache_ref.at[layer, pt_ref[b, p]], buf.at[slot], sem.at[slot])

    for p in range(n_buf):
        copy(p, p).start()
    km_scr[...] = jnp.zeros(km_scr.shape, F32)

    def block(j, carry):
        tot = None
        for u in range(pages_per_block):
            p = j * pages_per_block + u
            slot = p % n_buf
            copy(p, slot).wait()
            x = buf[slot].reshape(MOBA_W, page)
            tot = x if tot is None else tot + x

            @pl.when(p + n_buf < n_pages)
            def _():
                copy(p + n_buf, slot).start()

        km_scr[pl.ds(j, 1), :] = jnp.sum(tot.T, axis=0, keepdims=True) * (1.0 / (pages_per_block * page))
        return carry

    lax.fori_loop(0, nblk, block, 0)

    km = km_scr[...]
    nrow = km.shape[0]
    head_sum = (lax.broadcasted_iota(jnp.int32, (MOBA_W, 128), 0) // HEAD_DIM
                == lax.broadcasted_iota(jnp.int32, (MOBA_W, 128), 1))
    head_sum = jnp.where(head_sum, 1.0, 0.0).astype(BF16)
    rown = lax.broadcasted_iota(jnp.int32, (nrow, 128), 0)
    rown_f = rown.astype(F32)
    sel_ref[0] = jnp.zeros(sel_ref.shape[1:], jnp.int32)
    for t in range(q_ref.shape[1]):
        gate = _rdot3(km * q_ref[0, t:t + 1, :], head_sum)
        g = jnp.where(rown < nblk, gate, -jnp.inf)
        for r in range(MOBA_TOPK):
            mx = jnp.max(g, axis=0, keepdims=True)
            idx = jnp.min(jnp.where(g == mx, rown_f, float(nrow)), axis=0, keepdims=True)
            sel_ref[0, 4 * t + r:4 * t + r + 1, :] = idx.astype(jnp.int32)
            g = jnp.where(rown_f == idx, -jnp.inf, g)


def _sample_select(page_table, q_tb, cache_t, layer):
    nbat, n_pages = page_table.shape
    n_t = q_tb.shape[1]
    page = cache_t.shape[-1]
    ppb = MOBA_BLOCK // page
    nblk = n_pages // ppb
    assert n_pages % ppb == 0
    n_buf = min(N_PAGE_BUF, n_pages)
    sel_rows = -(-4 * n_t // 8) * 8
    return pl.pallas_call(
        functools.partial(_sample_select_kernel, layer=layer, n_pages=n_pages, pages_per_block=ppb, n_buf=n_buf),
        grid_spec=pltpu.PrefetchScalarGridSpec(
            num_scalar_prefetch=1,
            grid=(nbat,),
            in_specs=[pl.BlockSpec((1, n_t, MOBA_W), lambda b, pt: (b, 0, 0)),
                      pl.BlockSpec(memory_space=pl.ANY)],
            out_specs=pl.BlockSpec((1, sel_rows, 128), lambda b, pt: (b, 0, 0)),
            scratch_shapes=[pltpu.VMEM((n_buf, H_MOBA, HEAD_DIM, page), F32),
                            pltpu.SemaphoreType.DMA((n_buf,)),
                            pltpu.VMEM((-(-nblk // 8) * 8, MOBA_W), F32)],
        ),
        out_shape=jax.ShapeDtypeStruct((nbat, sel_rows, 128), jnp.int32),
        compiler_params=_cparams("arbitrary"),
        name="sample_select",
    )(page_table, q_tb, cache_t)


def _sample_attend_kernel(pid_ref, sel_ref, q_ref, kn_ref, vn_ref, bias_ref, ck_ref, cv_ref, o_ref,
                          kbuf, vbuf, sem, *, layer, n_t, pages_per_block, nblk):
    b = pl.program_id(0)
    nbat = pl.num_programs(0)
    page = ck_ref.shape[-1]
    n_slot = n_t * MOBA_TOPK

    def copies(bb, h, par):
        out = []
        for slot in range(n_slot):
            for u in range(pages_per_block):
                pid = pid_ref[bb, (h * n_slot + slot) * pages_per_block + u]
                lanes = pl.ds(u * page, page)
                out.append(pltpu.make_async_copy(ck_ref.at[layer, pid, h], kbuf.at[par, slot, :, lanes], sem.at[par, 0]))
                out.append(pltpu.make_async_copy(cv_ref.at[layer, pid, h], vbuf.at[par, slot, :, lanes], sem.at[par, 1]))
        return out

    @pl.when(b == 0)
    def _():
        for cp in copies(b, 0, 0):
            cp.start()

    causal_own = lax.broadcasted_iota(jnp.int32, (n_t, n_t), 1) <= lax.broadcasted_iota(jnp.int32, (n_t, n_t), 0)
    for h in range(H_MOBA):
        par = h % 2
        if h + 1 < H_MOBA:
            for cp in copies(b, h + 1, 1 - par):
                cp.start()
        else:
            @pl.when(b + 1 < nbat)
            def _():
                for cp in copies(b + 1, 0, 1 - par):
                    cp.start()

        for cp in copies(b, h, par):
            cp.wait()
        rows = slice(h * HEAD_DIM, (h + 1) * HEAD_DIM)
        qh = q_ref[0, rows, :] * (HEAD_DIM ** -0.5)
        kn = kn_ref[0, rows, :]
        vn = vn_ref[0, rows, :]
        s_rows, own_rows = [], []
        for t in range(n_t):
            qb = jnp.broadcast_to(qh[:, t:t + 1], (HEAD_DIM, MOBA_BLOCK))
            own_rows.append(jnp.sum(qh[:, t:t + 1] * kn, axis=0, keepdims=True) + bias_ref[h, n_t + 1 + t:n_t + 2 + t, 0:n_t])
            for r in range(MOBA_TOPK):
                slot = t * MOBA_TOPK + r
                j = sel_ref[b, h * n_slot + slot]
                prod = qb * kbuf[par, slot]
                s = jnp.sum(prod.reshape(HEAD_DIM // 8, 8, MOBA_BLOCK).sum(axis=0), axis=0, keepdims=True)
                s_rows.append(s + jnp.where(j == nblk - 1, bias_ref[h, t:t + 1, :], bias_ref[h, n_t:n_t + 1, :]))
        s_all = jnp.concatenate(s_rows, axis=0)
        s_own = jnp.where(causal_own, jnp.concatenate(own_rows, axis=0), MASK_NEG)
        m_slot = jnp.max(s_all, axis=1, keepdims=True)
        m_t = jnp.max(s_own, axis=1, keepdims=True)
        m_rows = []
        for t in range(n_t):
            mt = m_t[t:t + 1]
            for r in range(MOBA_TOPK):
                mt = jnp.maximum(mt, m_slot[t * MOBA_TOPK + r:t * MOBA_TOPK + r + 1])
            m_rows.append(mt)
        m_t = jnp.concatenate(m_rows, axis=0)
        m_rep = jnp.concatenate([m_rows[t] for t in range(n_t) for _ in range(MOBA_TOPK)], axis=0)
        p_all = jnp.exp(s_all - m_rep)
        p_own = jnp.exp(s_own - m_t)
        l_slot = jnp.sum(p_all, axis=1, keepdims=True)
        l_own = jnp.sum(p_own, axis=1, keepdims=True)
        for t in range(n_t):
            wsum = None
            l = l_own[t:t + 1]
            for r in range(MOBA_TOPK):
                slot = t * MOBA_TOPK + r
                l = l + l_slot[slot:slot + 1]
                term = p_all[slot:slot + 1] * vbuf[par, slot]
                wsum = term if wsum is None else wsum + term
            acc = jnp.sum(wsum, axis=1, keepdims=True) + jnp.sum(p_own[t:t + 1] * vn, axis=1, keepdims=True)
            o_ref[0, rows, t:t + 1] = acc / l


def _sample_attend(pids, sel, q_bt, kn_bt, vn_bt, bias_rows, cache_kt, cache_vt, layer, nblk):
    nbat = pids.shape[0]
    n_t = q_bt.shape[-1]
    page = cache_kt.shape[-1]
    ppb = MOBA_BLOCK // page
    n_slot = n_t * MOBA_TOPK
    new = pl.BlockSpec((1, MOBA_W, n_t), lambda b, pd, sl: (b, 0, 0))
    return pl.pallas_call(
        functools.partial(_sample_attend_kernel, layer=layer, n_t=n_t, pages_per_block=ppb, nblk=nblk),
        grid_spec=pltpu.PrefetchScalarGridSpec(
            num_scalar_prefetch=2,
            grid=(nbat,),
            in_specs=[new, new, new,
                      pl.BlockSpec(bias_rows.shape, lambda b, pd, sl: (0, 0, 0)),
                      pl.BlockSpec(memory_space=pl.ANY), pl.BlockSpec(memory_space=pl.ANY)],
            out_specs=new,
            scratch_shapes=[pltpu.VMEM((2, n_slot, HEAD_DIM, MOBA_BLOCK), F32),
                            pltpu.VMEM((2, n_slot, HEAD_DIM, MOBA_BLOCK), F32),
                            pltpu.SemaphoreType.DMA((2, 2))],
        ),
        out_shape=jax.ShapeDtypeStruct((nbat, MOBA_W, n_t), F32),
        compiler_params=_cparams("arbitrary"),
        name="sample_attend",
    )(pids, sel, q_bt, kn_bt, vn_bt, bias_rows, cache_kt, cache_vt)


def _mlp_kernel(x_ref, olin_ref, omoba_ref, wl_ref, wm_ref, gpost_ref, fpre_ref, wg_ref, wu_ref, wd_ref, fpost_ref, y_ref):
    mix = _dot(olin_ref[...], wl_ref[...]) + _dot(omoba_ref[...], wm_ref[...])
    x1 = x_ref[...] + _rms(mix, gpost_ref[...])
    h = _rms(x1, fpre_ref[...]).astype(BF16)
    gate = _dot(h, wg_ref[...])
    up = _dot(h, wu_ref[...])
    act = (gate / (1.0 + jnp.exp(-gate)) * up).astype(BF16)
    y_ref[...] = x1 + _rms(_dot(act, wd_ref[...]), fpost_ref[...])


def _mlp(x, olin, omoba, w_lin, w_moba, gpost, fpre, wg, wu, wd, fpost, tm):
    n, d = x.shape
    const = lambda a: pl.BlockSpec(a.shape, lambda i: (0,) * a.ndim, pipeline_mode=pl.Buffered(1))
    rows = lambda w: pl.BlockSpec((tm, w), lambda i: (i, 0))
    return pl.pallas_call(
        _mlp_kernel,
        grid=(n // tm,),
        in_specs=[rows(d), rows(olin.shape[1]), rows(omoba.shape[1]), const(w_lin), const(w_moba), const(gpost),
                  const(fpre), const(wg), const(wu), const(wd), const(fpost)],
        out_specs=rows(d),
        out_shape=jax.ShapeDtypeStruct((n, d), F32),
        compiler_params=_cparams("parallel"),
        name="mlp",
    )(x, olin, omoba, w_lin, w_moba, gpost, fpre, wg, wu, wd, fpost)


def _t5_bucket_np(dist):
    n = np.maximum(dist, 0)
    max_exact = N_BUCKETS // 2
    nf = np.maximum(n, 1).astype(np.float32)
    large = max_exact + (np.log(nf / np.float32(max_exact)) / np.float32(math.log(MAX_DISTANCE / max_exact))
                         * np.float32(N_BUCKETS - max_exact)).astype(np.int32)
    large = np.minimum(large, N_BUCKETS - 1)
    return np.where(n < max_exact, n, large).astype(np.int32)


def _rotary_tables(pos):
    half = LIN_KEY_DIM // 2
    inv = 1.0 / (ROPE_BASE ** jnp.linspace(0.0, 1.0, half, dtype=F32))
    ang = pos.astype(F32)[:, None] * inv[None, :]
    tile = lambda a: jnp.tile(a, (1, 2 * N_LIN_HEADS))
    return tile(jnp.cos(ang)), tile(jnp.sin(ang))


def _blockdiag_state(s):
    eye = jnp.eye(N_LIN_HEADS, dtype=s.dtype)
    return (s[:, :, :, None, :] * eye[None, :, None, :, None]).reshape(s.shape[0], LIN_QK_W, LIN_V_W)


def _head_states(sb):
    x = sb.reshape(sb.shape[0], N_LIN_HEADS, LIN_KEY_DIM, N_LIN_HEADS, HEAD_DIM)
    return jnp.stack([x[:, h, :, h, :] for h in range(N_LIN_HEADS)], axis=1)


def _row_tile(n, want):
    t = min(n, want)
    while n % t:
        t //= 2
    return t


def kernel(x_prompt, x_sample, cache_k, cache_v, state_ret, state_gla, page_table, w_in, w_out, gla_gate_w2, gla_gate_b, ret_norm_g, gla_norm_g, norm_attn_pre, norm_attn_post, norm_ffn_pre, norm_ffn_post, w_ffn_gate, w_ffn_up, w_ffn_down, rel_bias):
    depth = w_in.shape[0]
    nbp_, t_p, d_model = x_prompt.shape
    nbs, t_s, _ = x_sample.shape
    n_pages = page_table.shape[1]
    page = cache_k.shape[2]
    past_len = n_pages * page
    assert past_len % MOBA_BLOCK == 0 and t_s <= MOBA_BLOCK and t_p % MOBA_BLOCK == 0
    dff = w_ffn_gate.shape[2]

    cos_p, sin_p = _rotary_tables(jnp.arange(t_p, dtype=jnp.int32))
    cos_s, sin_s = _rotary_tables(past_len + jnp.arange(t_s, dtype=jnp.int32))
    off = np.arange(MOBA_BLOCK)
    dist_own = off[None, :] - off[:, None]
    bucket_near = np.concatenate([_t5_bucket_np(MOBA_BLOCK + dist_own),
                                  np.where(dist_own >= 0, _t5_bucket_np(dist_own), -1)], axis=0).astype(np.int32)
    bias_h = rel_bias.astype(F32).T
    bias_flat = bias_h.reshape(-1)
    far = bias_h[:, N_BUCKETS - 1]
    prev_rows = bias_h[:, _t5_bucket_np(MOBA_BLOCK + np.arange(t_s)[:, None] - off[None, :])]
    far_row = jnp.broadcast_to(far[:, None, None], (H_MOBA, 1, MOBA_BLOCK))
    own_rows = bias_h[:, _t5_bucket_np(np.arange(t_s)[:, None] - off[None, :])]
    n_rows = 2 * t_s + 1
    pad_rows = -(-n_rows // 8) * 8 - n_rows
    bias_rows = jnp.concatenate([prev_rows, far_row, own_rows, jnp.zeros((H_MOBA, pad_rows, MOBA_BLOCK), F32)], axis=1)

    rdec = jnp.repeat(jnp.log(1.0 - 2.0 ** (-5.0 - jnp.arange(N_LIN_HEADS, dtype=F32))), LIN_KEY_DIM)[None, :]
    cache_kt = jnp.transpose(cache_k, (0, 1, 3, 4, 2))
    cache_vt = jnp.transpose(cache_v, (0, 1, 3, 4, 2))
    w_in_t = jnp.swapaxes(w_in, 1, 2)

    hp = x_prompt
    hs = x_sample.reshape(1, nbs * t_s, d_model)
    zero_state = jnp.zeros((nbp_, 2, LIN_QK_W, LIN_V_W), F32)
    tm_p = _row_tile(t_p, 512)
    tb_p = _row_tile(t_p, 128)
    c_p = math.gcd(t_p, REC_CHUNK)
    c_s = math.gcd(t_s, REC_CHUNK)
    outs = {k: [] for k in ("kp", "vp", "rp", "gp", "ks", "vs", "rs", "gs")}

    for l in range(depth):
        seg = w_in_t[l]
        wtok = jnp.concatenate([seg[:LIN_TOK_W + GLA_RANK], jnp.zeros((GA_PAD - GLA_RANK, d_model), F32)], axis=0).T.astype(BF16)
        m0 = LIN_TOK_W + GLA_RANK
        wq_t = seg[m0:m0 + MOBA_W].astype(BF16)
        wk_t = seg[m0 + MOBA_W:m0 + 2 * MOBA_W].astype(BF16)
        wv_t = seg[m0 + 2 * MOBA_W:m0 + 3 * MOBA_W].astype(BF16)
        w2p = jnp.concatenate([gla_gate_w2[l], jnp.zeros((GA_PAD - GLA_RANK, LIN_QK_W), F32)], axis=0).astype(BF16)
        gb = gla_gate_b[l][None, :]
        rgain = jnp.tile(ret_norm_g[l], N_LIN_HEADS)[None, :]
        ggain = jnp.tile(gla_norm_g[l], N_LIN_HEADS)[None, :]
        w_o = w_out[l].astype(BF16)
        w_o_lin, w_o_moba = w_o[:2 * LIN_V_W], w_o[2 * LIN_V_W:]
        wg, wu, wd = w_ffn_gate[l].astype(BF16), w_ffn_up[l].astype(BF16), w_ffn_down[l].astype(BF16)
        g_pre, g_post = norm_attn_pre[l][None, :], norm_attn_post[l][None, :]
        f_pre, f_post = norm_ffn_pre[l][None, :], norm_ffn_post[l][None, :]

        tok, q_t, k_t, v_t = _inproj(hp, g_pre, wtok, wq_t, wk_t, wv_t, tm_p)
        olin, sfin = _linrec(tok, cos_p, sin_p, w2p, gb, rdec, rgain, ggain, zero_state, tb_p, c_p)
        omoba = _moba_prompt(q_t, k_t, v_t, bias_flat, bucket_near)
        hp = _mlp(hp.reshape(-1, d_model), olin.reshape(-1, 2 * LIN_V_W), omoba.reshape(-1, MOBA_W),
                  w_o_lin, w_o_moba, g_post, f_pre, wg, wu, wd, f_post, tm_p).reshape(nbp_, t_p, d_model)
        outs["kp"].append(jnp.transpose(k_t.reshape(nbp_, H_MOBA, HEAD_DIM, t_p), (0, 3, 1, 2)))
        outs["vp"].append(jnp.transpose(v_t.reshape(nbp_, H_MOBA, HEAD_DIM, t_p), (0, 3, 1, 2)))
        outs["rp"].append(_head_states(sfin[:, 0]))
        outs["gp"].append(_head_states(sfin[:, 1]))

        n_s = nbs * t_s
        tok, q_t, k_t, v_t = _inproj(hs, g_pre, wtok, wq_t, wk_t, wv_t, n_s)
        s0 = jnp.stack([_blockdiag_state(state_ret[l]), _blockdiag_state(state_gla[l])], axis=1)
        olin, sfin = _linrec(tok.reshape(nbs, t_s, TOK_W), cos_s, sin_s, w2p, gb, rdec, rgain, ggain, s0, t_s, c_s)
        per_batch = lambda a: jnp.transpose(a.reshape(MOBA_W, nbs, t_s), (1, 0, 2))
        q_bt, k_bt, v_bt = per_batch(q_t), per_batch(k_t), per_batch(v_t)
        sel = _sample_select(page_table, jnp.transpose(q_bt, (0, 2, 1)), cache_kt, l)
        sel = jnp.transpose(sel[:, :4 * t_s, :H_MOBA].reshape(nbs, t_s, 4, H_MOBA)[:, :, :MOBA_TOPK], (0, 3, 1, 2))
        sel = sel.reshape(nbs, -1)
        ppb = MOBA_BLOCK // page
        logical = (sel[:, :, None] * ppb + jnp.arange(ppb, dtype=jnp.int32)[None, None, :]).reshape(nbs, -1)
        hit = logical[:, :, None] == jnp.arange(n_pages, dtype=jnp.int32)[None, None, :]
        pids = jnp.sum(jnp.where(hit, page_table[:, None, :], 0), axis=-1)
        o_bt = _sample_attend(pids, sel, q_bt, k_bt, v_bt, bias_rows, cache_kt, cache_vt, l, n_pages // ppb)
        omoba = jnp.transpose(o_bt, (0, 2, 1)).reshape(n_s, MOBA_W).astype(BF16)
        hs = _mlp(hs.reshape(n_s, d_model), olin.reshape(n_s, 2 * LIN_V_W), omoba, w_o_lin, w_o_moba, g_post,
                  f_pre, wg, wu, wd, f_post, n_s).reshape(1, n_s, d_model)
        to_cache = lambda a: jnp.transpose(a.reshape(H_MOBA, HEAD_DIM, nbs, t_s), (2, 3, 0, 1))
        outs["ks"].append(to_cache(k_t))
        outs["vs"].append(to_cache(v_t))
        outs["rs"].append(_head_states(sfin[:, 0]))
        outs["gs"].append(_head_states(sfin[:, 1]))

    st = lambda k: jnp.stack(outs[k])
    return (hp, hs.reshape(nbs, t_s, d_model), st("kp"), st("vp"), st("rp"), st("gp"),
            st("ks"), st("vs"), st("rs"), st("gs"))
```

```python
import functools
import math

import jax
import jax.numpy as jnp
import numpy as np
from jax import lax
from jax.experimental import pallas as pl
from jax.experimental.pallas import tpu as pltpu

F32 = jnp.float32
BF16 = jnp.bfloat16

HEAD_DIM = 64
LIN_KEY_DIM = 32
N_LIN_HEADS = 4
H_MOBA = 8
LIN_QK_W = N_LIN_HEADS * LIN_KEY_DIM
LIN_V_W = N_LIN_HEADS * HEAD_DIM
MOBA_W = H_MOBA * HEAD_DIM
GLA_RANK = 16
GLA_TAU = 16.0
REC_CHUNK = 16
ROPE_BASE = 10000.0
MOBA_BLOCK = 256
MOBA_TOPK = 3
N_BUCKETS = 32
MAX_DISTANCE = 128
RMS_EPS = 1e-6
LIN_TOK_W = 2 * (2 * LIN_QK_W + 2 * LIN_V_W)
GA_PAD = 128
TOK_W = LIN_TOK_W + GA_PAD

VMEM_LIMIT_BYTES = 56 * 1024 * 1024
MASK_NEG = -1e30
LOG2E = 1.4426950408889634
V_ROWS = 80

_NT = (((1,), (1,)), ((), ()))
_TN = (((0,), (0,)), ((), ()))


def _cparams(*sem):
    return pltpu.CompilerParams(dimension_semantics=sem, vmem_limit_bytes=VMEM_LIMIT_BYTES)


def _rms(x, g):
    return x * lax.rsqrt(jnp.mean(x * x, axis=-1, keepdims=True) + RMS_EPS) * g


def _split3(x):
    hi = x.astype(BF16)
    r = x - hi.astype(F32)
    mid = r.astype(BF16)
    lo = (r - mid.astype(F32)).astype(BF16)
    return hi, mid, lo


def _dot(a, b):
    return jnp.dot(a, b, preferred_element_type=F32)


def _ldot3(a_bf16, x):
    hi, mid, lo = _split3(x)
    return _dot(a_bf16, hi) + _dot(a_bf16, mid) + _dot(a_bf16, lo)


def _rdot3(x, b_bf16):
    hi, mid, lo = _split3(x)
    return _dot(hi, b_bf16) + _dot(mid, b_bf16) + _dot(lo, b_bf16)


def _inproj_kernel(x_ref, g_ref, wtok_ref, wq_ref, wk_ref, wv_ref, tok_ref, q_ref, k_ref, v_ref):
    hb = _rms(x_ref[0], g_ref[...]).astype(BF16)
    tok_ref[0] = _dot(hb, wtok_ref[...])
    q_ref[0] = lax.dot_general(wq_ref[...], hb, _NT, preferred_element_type=F32)
    k_ref[0] = lax.dot_general(wk_ref[...], hb, _NT, preferred_element_type=F32)
    v_ref[0] = lax.dot_general(wv_ref[...], hb, _NT, preferred_element_type=F32)


def _inproj(x, g, wtok, wq_t, wk_t, wv_t, tm):
    nb, t, d = x.shape
    full = lambda s: pl.BlockSpec(s, lambda b, i: (0,) * len(s))
    chn = pl.BlockSpec((1, MOBA_W, tm), lambda b, i: (b, 0, i))
    return pl.pallas_call(
        _inproj_kernel,
        grid=(nb, t // tm),
        in_specs=[pl.BlockSpec((1, tm, d), lambda b, i: (b, i, 0)), full((1, d)), full((d, TOK_W)),
                  full((MOBA_W, d)), full((MOBA_W, d)), full((MOBA_W, d))],
        out_specs=[pl.BlockSpec((1, tm, TOK_W), lambda b, i: (b, i, 0)), chn, chn, chn],
        out_shape=[jax.ShapeDtypeStruct((nb, t, TOK_W), F32)] + [jax.ShapeDtypeStruct((nb, MOBA_W, t), F32)] * 3,
        compiler_params=_cparams("parallel", "parallel"),
        name="inproj",
    )(x, g, wtok, wq_t, wk_t, wv_t)


def _linrec_kernel(tok_ref, cos_ref, sin_ref, w2_ref, gb_ref, rdec_ref, rgain_ref, ggain_ref, s0_ref,
                   o_ref, sfin_ref, s_scr, *, tb, c, bpb):
    t = pl.program_id(1)

    @pl.when(t == 0)
    def _():
        s_scr[...] = s0_ref[...]

    n_sub = tb // c
    shift = int(math.log2(c))
    row = lax.broadcasted_iota(jnp.int32, (tb, tb), 0)
    col = lax.broadcasted_iota(jnp.int32, (tb, tb), 1)
    same = (row >> shift) == (col >> shift)
    tril = same & (col <= row)
    tril_b = jnp.where(tril, 1.0, 0.0).astype(BF16)
    same_b = jnp.where(same, 1.0, 0.0).astype(BF16)
    head_qk = lax.broadcasted_iota(jnp.int32, (1, LIN_QK_W), 1) // LIN_KEY_DIM
    head_v = lax.broadcasted_iota(jnp.int32, (1, LIN_V_W), 1) // HEAD_DIM
    blockdiag = (lax.broadcasted_iota(jnp.int32, (LIN_QK_W, LIN_V_W), 0) // LIN_KEY_DIM
                 == lax.broadcasted_iota(jnp.int32, (LIN_QK_W, LIN_V_W), 1) // HEAD_DIM)
    gi = lax.broadcasted_iota(jnp.int32, (LIN_V_W, LIN_V_W), 0) // HEAD_DIM
    gj = lax.broadcasted_iota(jnp.int32, (LIN_V_W, LIN_V_W), 1) // HEAD_DIM
    head_mean = jnp.where(gi == gj, 1.0 / HEAD_DIM, 0.0).astype(BF16)
    first_half = (lax.broadcasted_iota(jnp.int32, (1, LIN_QK_W), 1) % LIN_KEY_DIM) < LIN_KEY_DIM // 2
    cos = cos_ref[...]
    sin = sin_ref[...]
    qk_scale = LIN_KEY_DIM ** -0.5

    def rotary(x):
        rot = jnp.where(first_half, -pltpu.roll(x, LIN_QK_W - LIN_KEY_DIM // 2, 1), pltpu.roll(x, LIN_KEY_DIM // 2, 1))
        return x * cos + rot * sin

    sub_of_col = lax.broadcasted_iota(jnp.int32, (1, tb), 1) >> shift

    def group(bi, g_idx, q, k, v, b, b_end, gate, gain):
        q_dec = q * jnp.exp(b)
        k_inc = (k * jnp.exp(-b)).astype(BF16)
        k_dec = k * jnp.exp(b_end - b)
        b_end_t = b_end.T
        k_dec_t = k_dec.T
        vb = v.astype(BF16)
        q4 = jnp.concatenate([jnp.where(head_qk == h, q_dec, 0.0) for h in range(N_LIN_HEADS)], axis=0).astype(BF16)
        sc = lax.dot_general(q4, k_inc, _NT, preferred_element_type=F32)
        p4 = jnp.concatenate([jnp.where(tril, sc[h * tb:(h + 1) * tb], 0.0) for h in range(N_LIN_HEADS)], axis=0).astype(BF16)
        pv = _dot(p4, vb)
        o = jnp.where(head_v == 0, pv[0:tb], 0.0)
        for h in range(1, N_LIN_HEADS):
            o = o + jnp.where(head_v == h, pv[h * tb:(h + 1) * tb], 0.0)
        if n_sub == 1:
            u_all = lax.dot_general(k_dec.astype(BF16), vb, _TN, preferred_element_type=F32)
        else:
            k_stack = jnp.concatenate([jnp.where(sub_of_col == i, k_dec_t, 0.0) for i in range(n_sub)], axis=0)
            u_all = _dot(k_stack.astype(BF16), vb)
        s = s_scr[bi, g_idx]
        inter = []
        for i in range(n_sub):
            lo, hi = i * c, (i + 1) * c
            inter.append(_dot(q_dec[lo:hi].astype(BF16), s.astype(BF16)))
            u = u_all[i * LIN_QK_W:(i + 1) * LIN_QK_W]
            s = jnp.exp(b_end_t[:, lo:lo + 1]) * s + jnp.where(blockdiag, u, 0.0)
        s_scr[bi, g_idx] = s
        o = o + (inter[0] if n_sub == 1 else jnp.concatenate(inter, axis=0))
        ms = _rdot3(o * o, head_mean)
        on = o * lax.rsqrt(ms + RMS_EPS) * gain
        return on * (gate / (1.0 + jnp.exp(-gate)))

    a = LIN_QK_W
    w = LIN_V_W
    base = 2 * a + 2 * w
    for bi in range(bpb):
        rq = rotary(tok_ref[bi, :, 0:a])
        rk = rotary(tok_ref[bi, :, a:2 * a]) * qk_scale
        rv = tok_ref[bi, :, 2 * a:2 * a + w]
        rgate = tok_ref[bi, :, 2 * a + w:2 * a + 2 * w]
        pos_in_chunk = (lax.broadcasted_iota(jnp.int32, (tb, 1), 0) & (c - 1)).astype(F32)
        ret_b = (pos_in_chunk + 1.0) * rdec_ref[...]
        ret_b_end = jnp.broadcast_to(float(c) * rdec_ref[...], (tb, LIN_QK_W))
        o_ref[bi, :, 0:w] = group(bi, 0, rq, rk, rv, ret_b, ret_b_end, rgate, rgain_ref[...]).astype(o_ref.dtype)

        gq = tok_ref[bi, :, base:base + a] * qk_scale
        gk = tok_ref[bi, :, base + a:base + 2 * a]
        gv = tok_ref[bi, :, base + 2 * a:base + 2 * a + w]
        ggate = tok_ref[bi, :, base + 2 * a + w:base + 2 * a + 2 * w]
        ga = tok_ref[bi, :, LIN_TOK_W:TOK_W]
        z = _dot(ga.astype(BF16), w2_ref[...]) + gb_ref[...]
        gla_lg = (jnp.minimum(z, 0.0) - jnp.log(1.0 + jnp.exp(-jnp.abs(z)))) * (1.0 / GLA_TAU)
        gla_b = _ldot3(tril_b, gla_lg)
        gla_b_end = _ldot3(same_b, gla_lg)
        o_ref[bi, :, w:2 * w] = group(bi, 1, gq, gk, gv, gla_b, gla_b_end, ggate, ggain_ref[...]).astype(o_ref.dtype)

    @pl.when(t == pl.num_programs(1) - 1)
    def _():
        sfin_ref[...] = s_scr[...]


def _linrec(tok, cos, sin, w2p, gb, rdec, rgain, ggain, s0, tb, c):
    nb, t, _ = tok.shape
    bpb = 2 if nb % 2 == 0 else 1
    full = lambda s: pl.BlockSpec(s, lambda b, i: (0,) * len(s))
    st = pl.BlockSpec((bpb, 2, LIN_QK_W, LIN_V_W), lambda b, i: (b, 0, 0, 0))
    return pl.pallas_call(
        functools.partial(_linrec_kernel, tb=tb, c=c, bpb=bpb),
        grid=(nb // bpb, t // tb),
        in_specs=[pl.BlockSpec((bpb, tb, TOK_W), lambda b, i: (b, i, 0)),
                  pl.BlockSpec((tb, LIN_QK_W), lambda b, i: (i, 0)),
                  pl.BlockSpec((tb, LIN_QK_W), lambda b, i: (i, 0)),
                  full((GA_PAD, LIN_QK_W)), full((1, LIN_QK_W)), full((1, LIN_QK_W)),
                  full((1, LIN_V_W)), full((1, LIN_V_W)), st],
        out_specs=[pl.BlockSpec((bpb, tb, 2 * LIN_V_W), lambda b, i: (b, i, 0)), st],
        out_shape=[jax.ShapeDtypeStruct((nb, t, 2 * LIN_V_W), BF16),
                   jax.ShapeDtypeStruct((nb, 2, LIN_QK_W, LIN_V_W), F32)],
        scratch_shapes=[pltpu.VMEM((bpb, 2, LIN_QK_W, LIN_V_W), F32)],
        compiler_params=_cparams("parallel", "arbitrary"),
        name="linrec",
    )(tok, cos, sin, w2p, gb, rdec, rgain, ggain, s0)


def _near_bias_kernel(bias_ref, bucket_ref, tab_ref):
    head = pl.program_id(0)
    far_b = bias_ref[head * N_BUCKETS + N_BUCKETS - 1]
    for c0 in range(0, bucket_ref.shape[0], 128):
        bk = bucket_ref[c0:c0 + 128, :]

        def pick(bi, tab):
            return jnp.where(bk == bi, (bias_ref[head * N_BUCKETS + bi] - far_b) * LOG2E, tab)

        tab_ref[0, c0:c0 + 128, :] = lax.fori_loop(0, N_BUCKETS, pick, jnp.where(bk < 0, MASK_NEG, 0.0))


def _near_bias_table(bias_flat, bucket_near):
    rows, cols = bucket_near.shape
    return pl.pallas_call(
        _near_bias_kernel,
        grid_spec=pltpu.PrefetchScalarGridSpec(
            num_scalar_prefetch=1,
            grid=(H_MOBA,),
            in_specs=[pl.BlockSpec((rows, cols), lambda h, bias: (0, 0))],
            out_specs=pl.BlockSpec((1, rows, cols), lambda h, bias: (h, 0, 0)),
        ),
        out_shape=jax.ShapeDtypeStruct((H_MOBA, rows, cols), F32),
        compiler_params=_cparams("parallel"),
        name="near_bias",
    )(bias_flat, bucket_near)


def _select_topk(gate_t, n_valid, nbp, width):
    rown = lax.broadcasted_iota(jnp.int32, (nbp, width), 0)
    rown_f = rown.astype(F32)
    valid = rown < n_valid
    g = jnp.where(valid, gate_t, -jnp.inf)
    sel = jnp.zeros((nbp, width), F32)
    for _ in range(MOBA_TOPK):
        mx = jnp.max(g, axis=0, keepdims=True)
        idx = jnp.min(jnp.where(g == mx, rown_f, float(nbp)), axis=0, keepdims=True)
        pick = rown_f == idx
        sel = jnp.where(pick, jnp.where(valid, 1.0, 0.0), sel)
        g = jnp.where(pick, -jnp.inf, g)
    return sel, rown


def _moba_prompt_kernel(bias_ref, q_ref, k_ref, v_ref, near_ref, o_ref,
                        kaug_scr, vaug_scr, kmean_scr, s_scr, p_scr, s2_scr, *, nb, nbp):
    hp = pl.program_id(1)
    qi = pl.program_id(2)
    blk = MOBA_BLOCK
    n_tail = HEAD_DIM - nbp
    c_far, c_dummy = nbp, nbp + 2

    @pl.when(qi == 0)
    def _():
        lane = lax.broadcasted_iota(jnp.int32, (blk, HEAD_DIM), 1)
        lane_aug = lax.broadcasted_iota(jnp.int32, (blk, 2 * HEAD_DIM), 1)
        ones_row = jnp.where(lax.broadcasted_iota(jnp.int32, (V_ROWS - HEAD_DIM, blk), 0) == 0, 1.0, 0.0).astype(BF16)
        for hh in range(2):
            rows = slice(hh * HEAD_DIM, (hh + 1) * HEAD_DIM)
            for d0 in (0, (nb + 1) * blk, (nb + 2) * blk):
                kaug_scr[hh, d0:d0 + blk, :] = jnp.where(lane_aug == HEAD_DIM + c_dummy, 1.0, 0.0).astype(BF16)
                vaug_scr[hh, 0:HEAD_DIM, d0:d0 + blk] = jnp.zeros((HEAD_DIM, blk), BF16)
                vaug_scr[hh, HEAD_DIM:V_ROWS, d0:d0 + blk] = ones_row
            kmean_scr[hh] = jnp.zeros((nbp, HEAD_DIM), F32)

            def fill(n, carry):
                off = pl.multiple_of(n * blk, blk)
                dst = pl.ds(pl.multiple_of((n + 1) * blk, blk), blk)
                kb = k_ref[0, rows, pl.ds(off, blk)].T
                kmean_scr[hh, pl.ds(n, 1), :] = jnp.mean(kb, axis=0, keepdims=True)
                extra = jnp.where((lane == n) | (lane == c_far) | (lane == c_far + 1), 1.0, 0.0)
                kaug_scr[hh, dst, :] = jnp.concatenate([kb, extra], axis=1).astype(BF16)
                vaug_scr[hh, 0:HEAD_DIM, dst] = v_ref[0, rows, pl.ds(off, blk)].astype(BF16)
                vaug_scr[hh, HEAD_DIM:V_ROWS, dst] = ones_row
                return carry

            lax.fori_loop(0, nb, fill, 0)

    tail_row = lax.broadcasted_iota(jnp.int32, (n_tail, blk), 0)
    near_off = pl.multiple_of(qi * blk, blk)
    state = []
    q_far = []
    for hh in range(2):
        head = hp * 2 + hh
        q_t = q_ref[0, hh * HEAD_DIM:(hh + 1) * HEAD_DIM, :]
        gate_t = jnp.dot(kmean_scr[hh], q_t, preferred_element_type=F32, precision=lax.Precision.HIGHEST)
        sel, rown = _select_topk(gate_t, qi, nbp, blk)
        near_ok = (rown == qi) | ((rown == qi - 1) & (sel > 0.0))
        far_ok = (rown < qi - 1) & (sel > 0.0)
        far_v = jnp.full((n_tail, blk), bias_ref[head * N_BUCKETS + N_BUCKETS - 1] * LOG2E, F32)
        far_hi = far_v.astype(BF16).astype(F32)
        tail = jnp.where(tail_row == 0, far_hi, jnp.where(tail_row == 1, far_v - far_hi,
                                                          jnp.where(tail_row == 2, MASK_NEG, 0.0)))
        qs = q_t * (HEAD_DIM ** -0.5 * LOG2E)
        q_near = jnp.concatenate([qs, jnp.where(near_ok, 0.0, MASK_NEG), tail], axis=0).astype(BF16)
        q_far.append(jnp.concatenate([qs, jnp.where(far_ok, 0.0, MASK_NEG), tail], axis=0).astype(BF16))
        s_scr[hh] = _dot(kaug_scr[hh, pl.ds(near_off, 2 * blk), :], q_near) + near_ref[hh]
    ck = 256
    n_ck = 2 * blk // ck
    n_it = (qi + 1) // 2

    def qk_tile(t, buf):
        off = pl.multiple_of(t * 2 * blk, 2 * blk)
        maxes = []
        for hh in range(2):
            mloc = None
            for c in range(n_ck):
                sc = _dot(kaug_scr[hh, pl.ds(off + c * ck, ck), :], q_far[hh])
                s2_scr[buf, hh, c * ck:(c + 1) * ck, :] = sc
                cm = jnp.max(sc.reshape(ck // 8, 8, blk), axis=0)
                mloc = cm if mloc is None else jnp.maximum(mloc, cm)
            maxes.append(jnp.max(mloc, axis=0, keepdims=True))
        return maxes

    first_max = qk_tile(0, 0)

    for hh in range(2):
        m = jnp.max(s_scr[hh], axis=0, keepdims=True)
        p_scr[hh] = jnp.exp2(s_scr[hh] - m).astype(BF16)
        state += [m, _dot(vaug_scr[hh, :, pl.ds(near_off, 2 * blk)], p_scr[hh])]

    def soft_pv(t, buf, maxes, st):
        off = pl.multiple_of(t * 2 * blk, 2 * blk)
        out = []
        for hh in range(2):
            m, acc = st[2 * hh], st[2 * hh + 1]
            m_new = jnp.maximum(m, maxes[hh])
            for c in range(n_ck):
                rows_c = slice(c * ck, (c + 1) * ck)
                p_scr[hh, rows_c, :] = jnp.exp2(s2_scr[buf, hh, rows_c, :] - m_new).astype(BF16)
            acc = acc * jnp.exp2(m - m_new) + _dot(vaug_scr[hh, :, pl.ds(off, 2 * blk)], p_scr[hh])
            out += [m_new, acc]
        return out

    def far_pair(j, carry):
        st, max_a = list(carry[:4]), list(carry[4:])
        max_b = qk_tile(2 * j + 1, 1)
        st = soft_pv(2 * j, 0, max_a, st)
        max_a = qk_tile(jnp.minimum(2 * j + 2, n_it), 0)
        st = soft_pv(2 * j + 1, 1, max_b, st)
        return tuple(st + max_a)

    carry = lax.fori_loop(0, (n_it + 1) // 2, far_pair, tuple(state + first_max))
    state = carry[:4]
    for hh in range(2):
        acc = state[2 * hh + 1]
        o = acc[0:HEAD_DIM] / acc[HEAD_DIM:HEAD_DIM + 1]
        o_ref[0, :, hh * HEAD_DIM:(hh + 1) * HEAD_DIM] = o.T.astype(o_ref.dtype)


def _moba_prompt(q_t, k_t, v_t, bias_flat, near_tab):
    nbat, _, t = q_t.shape
    nb = t // MOBA_BLOCK
    nbp = max(8, -(-nb // 8) * 8)
    assert t % MOBA_BLOCK == 0 and nbp + 3 <= HEAD_DIM
    kv = pl.BlockSpec((1, 2 * HEAD_DIM, t), lambda b, hp, qi, bias: (b, hp, 0))
    return pl.pallas_call(
        functools.partial(_moba_prompt_kernel, nb=nb, nbp=nbp),
        grid_spec=pltpu.PrefetchScalarGridSpec(
            num_scalar_prefetch=1,
            grid=(nbat, H_MOBA // 2, nb),
            in_specs=[pl.BlockSpec((1, 2 * HEAD_DIM, MOBA_BLOCK), lambda b, hp, qi, bias: (b, hp, qi)), kv, kv,
                      pl.BlockSpec((2, 2 * MOBA_BLOCK, MOBA_BLOCK), lambda b, hp, qi, bias: (hp, 0, 0))],
            out_specs=pl.BlockSpec((1, MOBA_BLOCK, 2 * HEAD_DIM), lambda b, hp, qi, bias: (b, qi, hp)),
            scratch_shapes=[pltpu.VMEM((2, t + 3 * MOBA_BLOCK, 2 * HEAD_DIM), BF16),
                            pltpu.VMEM((2, V_ROWS, t + 3 * MOBA_BLOCK), BF16),
                            pltpu.VMEM((2, nbp, HEAD_DIM), F32),
                            pltpu.VMEM((2, 2 * MOBA_BLOCK, MOBA_BLOCK), F32),
                            pltpu.VMEM((2, 2 * MOBA_BLOCK, MOBA_BLOCK), BF16),
                            pltpu.VMEM((2, 2, 2 * MOBA_BLOCK, MOBA_BLOCK), F32)],
        ),
        out_shape=jax.ShapeDtypeStruct((nbat, t, MOBA_W), BF16),
        compiler_params=_cparams("parallel", "parallel", "arbitrary"),
        name="moba_prompt",
    )(bias_flat, q_t, k_t, v_t, near_tab)


N_PAGE_BUF = 32


def _sample_select_kernel(pt_ref, q_ref, cache_ref, sel_ref, buf, sem, km_scr, *, layer, n_pages, pages_per_block, n_buf):
    b = pl.program_id(0)
    nblk = n_pages // pages_per_block
    page = cache_ref.shape[-1]

    def copy(p, slot):
        return pltpu.make_async_copy(cache_ref.at[layer, pt_ref[b, p]], buf.at[slot], sem.at[slot])

    for p in range(n_buf):
        copy(p, p).start()
    km_scr[...] = jnp.zeros(km_scr.shape, F32)

    def block(j, carry):
        tot = None
        for u in range(pages_per_block):
            p = j * pages_per_block + u
            slot = p % n_buf
            copy(p, slot).wait()
            x = buf[slot].reshape(MOBA_W, page)
            tot = x if tot is None else tot + x

            @pl.when(p + n_buf < n_pages)
            def _():
                copy(p + n_buf, slot).start()

        km_scr[pl.ds(j, 1), :] = jnp.sum(tot.T, axis=0, keepdims=True) * (1.0 / (pages_per_block * page))
        return carry

    lax.fori_loop(0, nblk, block, 0)

    km = km_scr[...]
    nrow = km.shape[0]
    head_sum = (lax.broadcasted_iota(jnp.int32, (MOBA_W, 128), 0) // HEAD_DIM
                == lax.broadcasted_iota(jnp.int32, (MOBA_W, 128), 1))
    head_sum = jnp.where(head_sum, 1.0, 0.0).astype(BF16)
    rown = lax.broadcasted_iota(jnp.int32, (nrow, 128), 0)
    rown_f = rown.astype(F32)
    sel_ref[0] = jnp.zeros(sel_ref.shape[1:], jnp.int32)
    for t in range(q_ref.shape[1]):
        gate = _rdot3(km * q_ref[0, t:t + 1, :], head_sum)
        g = jnp.where(rown < nblk, gate, -jnp.inf)
        for r in range(MOBA_TOPK):
            mx = jnp.max(g, axis=0, keepdims=True)
            idx = jnp.min(jnp.where(g == mx, rown_f, float(nrow)), axis=0, keepdims=True)
            sel_ref[0, 4 * t + r:4 * t + r + 1, :] = idx.astype(jnp.int32)
            g = jnp.where(rown_f == idx, -jnp.inf, g)


def _sample_select(page_table, q_tb, cache_t, layer):
    nbat, n_pages = page_table.shape
    n_t = q_tb.shape[1]
    page = cache_t.shape[-1]
    ppb = MOBA_BLOCK // page
    nblk = n_pages // ppb
    assert n_pages % ppb == 0
    n_buf = min(N_PAGE_BUF, n_pages)
    sel_rows = -(-4 * n_t // 8) * 8
    return pl.pallas_call(
        functools.partial(_sample_select_kernel, layer=layer, n_pages=n_pages, pages_per_block=ppb, n_buf=n_buf),
        grid_spec=pltpu.PrefetchScalarGridSpec(
            num_scalar_prefetch=1,
            grid=(nbat,),
            in_specs=[pl.BlockSpec((1, n_t, MOBA_W), lambda b, pt: (b, 0, 0)),
                      pl.BlockSpec(memory_space=pl.ANY)],
            out_specs=pl.BlockSpec((1, sel_rows, 128), lambda b, pt: (b, 0, 0)),
            scratch_shapes=[pltpu.VMEM((n_buf, H_MOBA, HEAD_DIM, page), F32),
                            pltpu.SemaphoreType.DMA((n_buf,)),
                            pltpu.VMEM((-(-nblk // 8) * 8, MOBA_W), F32)],
        ),
        out_shape=jax.ShapeDtypeStruct((nbat, sel_rows, 128), jnp.int32),
        compiler_params=_cparams("arbitrary"),
        name="sample_select",
    )(page_table, q_tb, cache_t)


N_HEAD_BUF = 4
N_AHEAD = 2


def _sample_attend_kernel(pid_ref, sel_ref, q_ref, kn_ref, vn_ref, bias_ref, ck_ref, cv_ref, o_ref,
                          kbuf, vbuf, sem, *, layer, n_t, pages_per_block, nblk):
    b = pl.program_id(0)
    nbat = pl.num_programs(0)
    page = ck_ref.shape[-1]
    n_slot = n_t * MOBA_TOPK

    def copies(bb, h, par):
        out = []
        for slot in range(n_slot):
            for u in range(pages_per_block):
                pid = pid_ref[bb, (h * n_slot + slot) * pages_per_block + u]
                lanes = pl.ds(u * page, page)
                out.append(pltpu.make_async_copy(ck_ref.at[layer, pid, h], kbuf.at[par, slot, :, lanes], sem.at[par, 0]))
                out.append(pltpu.make_async_copy(cv_ref.at[layer, pid, h], vbuf.at[par, slot, :, lanes], sem.at[par, 1]))
        return out

    @pl.when(b == 0)
    def _():
        for h0 in range(N_AHEAD):
            for cp in copies(b, h0, h0):
                cp.start()

    causal_own = lax.broadcasted_iota(jnp.int32, (n_t, n_t), 1) <= lax.broadcasted_iota(jnp.int32, (n_t, n_t), 0)
    for h in range(H_MOBA):
        par = h % N_HEAD_BUF
        nxt = h + N_AHEAD
        if nxt < H_MOBA:
            for cp in copies(b, nxt, nxt % N_HEAD_BUF):
                cp.start()
        else:
            @pl.when(b + 1 < nbat)
            def _():
                for cp in copies(b + 1, nxt - H_MOBA, nxt % N_HEAD_BUF):
                    cp.start()

        for cp in copies(b, h, par):
            cp.wait()
        rows = slice(h * HEAD_DIM, (h + 1) * HEAD_DIM)
        qh = q_ref[0, rows, :] * (HEAD_DIM ** -0.5)
        kn = kn_ref[0, rows, :]
        vn = vn_ref[0, rows, :]
        s_rows, own_rows = [], []
        for t in range(n_t):
            qb = jnp.broadcast_to(qh[:, t:t + 1], (HEAD_DIM, MOBA_BLOCK))
            own_rows.append(jnp.sum(qh[:, t:t + 1] * kn, axis=0, keepdims=True) + bias_ref[h, n_t + 1 + t:n_t + 2 + t, 0:n_t])
            for r in range(MOBA_TOPK):
                slot = t * MOBA_TOPK + r
                j = sel_ref[b, h * n_slot + slot]
                prod = qb * kbuf[par, slot]
                s = jnp.sum(prod.reshape(HEAD_DIM // 8, 8, MOBA_BLOCK).sum(axis=0), axis=0, keepdims=True)
                s_rows.append(s + jnp.where(j == nblk - 1, bias_ref[h, t:t + 1, :], bias_ref[h, n_t:n_t + 1, :]))
        s_all = jnp.concatenate(s_rows, axis=0)
        s_own = jnp.where(causal_own, jnp.concatenate(own_rows, axis=0), MASK_NEG)
        m_slot = jnp.max(s_all, axis=1, keepdims=True)
        m_t = jnp.max(s_own, axis=1, keepdims=True)
        m_rows = []
        for t in range(n_t):
            mt = m_t[t:t + 1]
            for r in range(MOBA_TOPK):
                mt = jnp.maximum(mt, m_slot[t * MOBA_TOPK + r:t * MOBA_TOPK + r + 1])
            m_rows.append(mt)
        m_t = jnp.concatenate(m_rows, axis=0)
        m_rep = jnp.concatenate([m_rows[t] for t in range(n_t) for _ in range(MOBA_TOPK)], axis=0)
        p_all = jnp.exp(s_all - m_rep)
        p_own = jnp.exp(s_own - m_t)
        l_slot = jnp.sum(p_all, axis=1, keepdims=True)
        l_own = jnp.sum(p_own, axis=1, keepdims=True)
        for t in range(n_t):
            wsum = None
            l = l_own[t:t + 1]
            for r in range(MOBA_TOPK):
                slot = t * MOBA_TOPK + r
                l = l + l_slot[slot:slot + 1]
                term = p_all[slot:slot + 1] * vbuf[par, slot]
                wsum = term if wsum is None else wsum + term
            acc = jnp.sum(wsum, axis=1, keepdims=True) + jnp.sum(p_own[t:t + 1] * vn, axis=1, keepdims=True)
            o_ref[0, rows, t:t + 1] = acc / l


def _sample_attend(pids, sel, q_bt, kn_bt, vn_bt, bias_rows, cache_kt, cache_vt, layer, nblk):
    nbat = pids.shape[0]
    n_t = q_bt.shape[-1]
    page = cache_kt.shape[-1]
    ppb = MOBA_BLOCK // page
    n_slot = n_t * MOBA_TOPK
    new = pl.BlockSpec((1, MOBA_W, n_t), lambda b, pd, sl: (b, 0, 0))
    return pl.pallas_call(
        functools.partial(_sample_attend_kernel, layer=layer, n_t=n_t, pages_per_block=ppb, nblk=nblk),
        grid_spec=pltpu.PrefetchScalarGridSpec(
            num_scalar_prefetch=2,
            grid=(nbat,),
            in_specs=[new, new, new,
                      pl.BlockSpec(bias_rows.shape, lambda b, pd, sl: (0, 0, 0)),
                      pl.BlockSpec(memory_space=pl.ANY), pl.BlockSpec(memory_space=pl.ANY)],
            out_specs=new,
            scratch_shapes=[pltpu.VMEM((N_HEAD_BUF, n_slot, HEAD_DIM, MOBA_BLOCK), F32),
                            pltpu.VMEM((N_HEAD_BUF, n_slot, HEAD_DIM, MOBA_BLOCK), F32),
                            pltpu.SemaphoreType.DMA((N_HEAD_BUF, 2))],
        ),
        out_shape=jax.ShapeDtypeStruct((nbat, MOBA_W, n_t), F32),
        compiler_params=_cparams("arbitrary"),
        name="sample_attend",
    )(pids, sel, q_bt, kn_bt, vn_bt, bias_rows, cache_kt, cache_vt)


def _mlp_kernel(x_ref, olin_ref, omoba_ref, wl_ref, wm_ref, gpost_ref, fpre_ref, wg_ref, wu_ref, wd_ref, fpost_ref, y_ref):
    mix = _dot(olin_ref[...], wl_ref[...]) + _dot(omoba_ref[...], wm_ref[...])
    x1 = x_ref[...] + _rms(mix, gpost_ref[...])
    h = _rms(x1, fpre_ref[...]).astype(BF16)
    gate = _dot(h, wg_ref[...])
    up = _dot(h, wu_ref[...])
    act = (gate / (1.0 + jnp.exp(-gate)) * up).astype(BF16)
    y_ref[...] = x1 + _rms(_dot(act, wd_ref[...]), fpost_ref[...])


def _mlp(x, olin, omoba, w_lin, w_moba, gpost, fpre, wg, wu, wd, fpost, tm):
    n, d = x.shape
    const = lambda a: pl.BlockSpec(a.shape, lambda i: (0,) * a.ndim, pipeline_mode=pl.Buffered(1))
    rows = lambda w: pl.BlockSpec((tm, w), lambda i: (i, 0))
    return pl.pallas_call(
        _mlp_kernel,
        grid=(n // tm,),
        in_specs=[rows(d), rows(olin.shape[1]), rows(omoba.shape[1]), const(w_lin), const(w_moba), const(gpost),
                  const(fpre), const(wg), const(wu), const(wd), const(fpost)],
        out_specs=rows(d),
        out_shape=jax.ShapeDtypeStruct((n, d), F32),
        compiler_params=_cparams("parallel"),
        name="mlp",
    )(x, olin, omoba, w_lin, w_moba, gpost, fpre, wg, wu, wd, fpost)


def _t5_bucket_np(dist):
    n = np.maximum(dist, 0)
    max_exact = N_BUCKETS // 2
    nf = np.maximum(n, 1).astype(np.float32)
    large = max_exact + (np.log(nf / np.float32(max_exact)) / np.float32(math.log(MAX_DISTANCE / max_exact))
                         * np.float32(N_BUCKETS - max_exact)).astype(np.int32)
    large = np.minimum(large, N_BUCKETS - 1)
    return np.where(n < max_exact, n, large).astype(np.int32)


def _rotary_tables(pos):
    half = LIN_KEY_DIM // 2
    inv = 1.0 / (ROPE_BASE ** jnp.linspace(0.0, 1.0, half, dtype=F32))
    ang = pos.astype(F32)[:, None] * inv[None, :]
    tile = lambda a: jnp.tile(a, (1, 2 * N_LIN_HEADS))
    return tile(jnp.cos(ang)), tile(jnp.sin(ang))


def _blockdiag_state(s):
    eye = jnp.eye(N_LIN_HEADS, dtype=s.dtype)
    return (s[:, :, :, None, :] * eye[None, :, None, :, None]).reshape(s.shape[0], LIN_QK_W, LIN_V_W)


def _head_states(sb):
    x = sb.reshape(sb.shape[0], N_LIN_HEADS, LIN_KEY_DIM, N_LIN_HEADS, HEAD_DIM)
    return jnp.stack([x[:, h, :, h, :] for h in range(N_LIN_HEADS)], axis=1)


def _row_tile(n, want):
    t = min(n, want)
    while n % t:
        t //= 2
    return t


def kernel(x_prompt, x_sample, cache_k, cache_v, state_ret, state_gla, page_table, w_in, w_out, gla_gate_w2, gla_gate_b, ret_norm_g, gla_norm_g, norm_attn_pre, norm_attn_post, norm_ffn_pre, norm_ffn_post, w_ffn_gate, w_ffn_up, w_ffn_down, rel_bias):
    depth = w_in.shape[0]
    nbp_, t_p, d_model = x_prompt.shape
    nbs, t_s, _ = x_sample.shape
    n_pages = page_table.shape[1]
    page = cache_k.shape[2]
    past_len = n_pages * page
    assert past_len % MOBA_BLOCK == 0 and t_s <= MOBA_BLOCK and t_p % MOBA_BLOCK == 0
    dff = w_ffn_gate.shape[2]

    cos_p, sin_p = _rotary_tables(jnp.arange(t_p, dtype=jnp.int32))
    cos_s, sin_s = _rotary_tables(past_len + jnp.arange(t_s, dtype=jnp.int32))
    off = np.arange(MOBA_BLOCK)
    dist_own = off[None, :] - off[:, None]
    bucket_near = np.concatenate([_t5_bucket_np(MOBA_BLOCK + dist_own),
                                  np.where(dist_own >= 0, _t5_bucket_np(dist_own), -1)], axis=0).astype(np.int32)
    bias_h = rel_bias.astype(F32).T
    bias_flat = bias_h.reshape(-1)
    near_tab = _near_bias_table(bias_flat, bucket_near)
    far = bias_h[:, N_BUCKETS - 1]
    prev_rows = bias_h[:, _t5_bucket_np(MOBA_BLOCK + np.arange(t_s)[:, None] - off[None, :])]
    far_row = jnp.broadcast_to(far[:, None, None], (H_MOBA, 1, MOBA_BLOCK))
    own_rows = bias_h[:, _t5_bucket_np(np.arange(t_s)[:, None] - off[None, :])]
    n_rows = 2 * t_s + 1
    pad_rows = -(-n_rows // 8) * 8 - n_rows
    bias_rows = jnp.concatenate([prev_rows, far_row, own_rows, jnp.zeros((H_MOBA, pad_rows, MOBA_BLOCK), F32)], axis=1)

    rdec = jnp.repeat(jnp.log(1.0 - 2.0 ** (-5.0 - jnp.arange(N_LIN_HEADS, dtype=F32))), LIN_KEY_DIM)[None, :]
    cache_kt = jnp.transpose(cache_k, (0, 1, 3, 4, 2))
    cache_vt = jnp.transpose(cache_v, (0, 1, 3, 4, 2))
    w_in_t = jnp.swapaxes(w_in, 1, 2)

    hp = x_prompt
    hs = x_sample.reshape(1, nbs * t_s, d_model)
    zero_state = jnp.zeros((nbp_, 2, LIN_QK_W, LIN_V_W), F32)
    tm_p = _row_tile(t_p, 512)
    tb_p = _row_tile(t_p, 256)
    c_p = math.gcd(t_p, REC_CHUNK)
    c_s = math.gcd(t_s, REC_CHUNK)
    outs = {k: [] for k in ("kp", "vp", "rp", "gp", "ks", "vs", "rs", "gs")}

    for l in range(depth):
        seg = w_in_t[l]
        wtok = jnp.concatenate([seg[:LIN_TOK_W + GLA_RANK], jnp.zeros((GA_PAD - GLA_RANK, d_model), F32)], axis=0).T.astype(BF16)
        m0 = LIN_TOK_W + GLA_RANK
        wq_t = seg[m0:m0 + MOBA_W].astype(BF16)
        wk_t = seg[m0 + MOBA_W:m0 + 2 * MOBA_W].astype(BF16)
        wv_t = seg[m0 + 2 * MOBA_W:m0 + 3 * MOBA_W].astype(BF16)
        w2p = jnp.concatenate([gla_gate_w2[l], jnp.zeros((GA_PAD - GLA_RANK, LIN_QK_W), F32)], axis=0).astype(BF16)
        gb = gla_gate_b[l][None, :]
        rgain = jnp.tile(ret_norm_g[l], N_LIN_HEADS)[None, :]
        ggain = jnp.tile(gla_norm_g[l], N_LIN_HEADS)[None, :]
        w_o = w_out[l].astype(BF16)
        w_o_lin, w_o_moba = w_o[:2 * LIN_V_W], w_o[2 * LIN_V_W:]
        wg, wu, wd = w_ffn_gate[l].astype(BF16), w_ffn_up[l].astype(BF16), w_ffn_down[l].astype(BF16)
        g_pre, g_post = norm_attn_pre[l][None, :], norm_attn_post[l][None, :]
        f_pre, f_post = norm_ffn_pre[l][None, :], norm_ffn_post[l][None, :]

        tok, q_t, k_t, v_t = _inproj(hp, g_pre, wtok, wq_t, wk_t, wv_t, tm_p)
        olin, sfin = _linrec(tok, cos_p, sin_p, w2p, gb, rdec, rgain, ggain, zero_state, tb_p, c_p)
        omoba = _moba_prompt(q_t, k_t, v_t, bias_flat, near_tab)
        hp = _mlp(hp.reshape(-1, d_model), olin.reshape(-1, 2 * LIN_V_W), omoba.reshape(-1, MOBA_W),
                  w_o_lin, w_o_moba, g_post, f_pre, wg, wu, wd, f_post, tm_p).reshape(nbp_, t_p, d_model)
        outs["kp"].append(jnp.transpose(k_t.reshape(nbp_, H_MOBA, HEAD_DIM, t_p), (0, 3, 1, 2)))
        outs["vp"].append(jnp.transpose(v_t.reshape(nbp_, H_MOBA, HEAD_DIM, t_p), (0, 3, 1, 2)))
        outs["rp"].append(_head_states(sfin[:, 0]))
        outs["gp"].append(_head_states(sfin[:, 1]))

        n_s = nbs * t_s
        tok, q_t, k_t, v_t = _inproj(hs, g_pre, wtok, wq_t, wk_t, wv_t, n_s)
        s0 = jnp.stack([_blockdiag_state(state_ret[l]), _blockdiag_state(state_gla[l])], axis=1)
        olin, sfin = _linrec(tok.reshape(nbs, t_s, TOK_W), cos_s, sin_s, w2p, gb, rdec, rgain, ggain, s0, t_s, c_s)
        per_batch = lambda a: jnp.transpose(a.reshape(MOBA_W, nbs, t_s), (1, 0, 2))
        q_bt, k_bt, v_bt = per_batch(q_t), per_batch(k_t), per_batch(v_t)
        sel = _sample_select(page_table, jnp.transpose(q_bt, (0, 2, 1)), cache_kt, l)
        sel = jnp.transpose(sel[:, :4 * t_s, :H_MOBA].reshape(nbs, t_s, 4, H_MOBA)[:, :, :MOBA_TOPK], (0, 3, 1, 2))
        sel = sel.reshape(nbs, -1)
        ppb = MOBA_BLOCK // page
        logical = (sel[:, :, None] * ppb + jnp.arange(ppb, dtype=jnp.int32)[None, None, :]).reshape(nbs, -1)
        hit = logical[:, :, None] == jnp.arange(n_pages, dtype=jnp.int32)[None, None, :]
        pids = jnp.sum(jnp.where(hit, page_table[:, None, :], 0), axis=-1)
        o_bt = _sample_attend(pids, sel, q_bt, k_bt, v_bt, bias_rows, cache_kt, cache_vt, l, n_pages // ppb)
        omoba = jnp.transpose(o_bt, (0, 2, 1)).reshape(n_s, MOBA_W).astype(BF16)
        hs = _mlp(hs.reshape(n_s, d_model), olin.reshape(n_s, 2 * LIN_V_W), omoba, w_o_lin, w_o_moba, g_post,
                  f_pre, wg, wu, wd, f_post, n_s).reshape(1, n_s, d_model)
        to_cache = lambda a: jnp.transpose(a.reshape(H_MOBA, HEAD_DIM, nbs, t_s), (2, 3, 0, 1))
        outs["ks"].append(to_cache(k_t))
        outs["vs"].append(to_cache(v_t))
        outs["rs"].append(_head_states(sfin[:, 0]))
        outs["gs"].append(_head_states(sfin[:, 1]))

    st = lambda k: jnp.stack(outs[k])
    return (hp, hs.reshape(nbs, t_s, d_model), st("kp"), st("vp"), st("rp"), st("gp"),
            st("ks"), st("vs"), st("rs"), st("gs"))
```

```python
import functools
import math

import jax
import jax.numpy as jnp
import numpy as np
from jax import lax
from jax.experimental import pallas as pl
from jax.experimental.pallas import tpu as pltpu

F32 = jnp.float32
BF16 = jnp.bfloat16

HEAD_DIM = 64
LIN_KEY_DIM = 32
N_LIN_HEADS = 4
H_MOBA = 8
LIN_QK_W = N_LIN_HEADS * LIN_KEY_DIM
LIN_V_W = N_LIN_HEADS * HEAD_DIM
MOBA_W = H_MOBA * HEAD_DIM
GLA_RANK = 16
GLA_TAU = 16.0
REC_CHUNK = 16
ROPE_BASE = 10000.0
MOBA_BLOCK = 256
MOBA_TOPK = 3
N_BUCKETS = 32
MAX_DISTANCE = 128
RMS_EPS = 1e-6
LIN_TOK_W = 2 * (2 * LIN_QK_W + 2 * LIN_V_W)
GA_PAD = 128
TOK_W = LIN_TOK_W + GA_PAD

VMEM_LIMIT_BYTES = 56 * 1024 * 1024
MASK_NEG = -1e30
LOG2E = 1.4426950408889634
V_ROWS = 80

_NT = (((1,), (1,)), ((), ()))
_TN = (((0,), (0,)), ((), ()))


def _cparams(*sem):
    return pltpu.CompilerParams(dimension_semantics=sem, vmem_limit_bytes=VMEM_LIMIT_BYTES)


def _rms(x, g):
    return x * lax.rsqrt(jnp.mean(x * x, axis=-1, keepdims=True) + RMS_EPS) * g


def _split3(x):
    hi = x.astype(BF16)
    r = x - hi.astype(F32)
    mid = r.astype(BF16)
    lo = (r - mid.astype(F32)).astype(BF16)
    return hi, mid, lo


def _dot(a, b):
    return jnp.dot(a, b, preferred_element_type=F32)


def _ldot3(a_bf16, x):
    hi, mid, lo = _split3(x)
    return _dot(a_bf16, hi) + _dot(a_bf16, mid) + _dot(a_bf16, lo)


def _rdot3(x, b_bf16):
    hi, mid, lo = _split3(x)
    return _dot(hi, b_bf16) + _dot(mid, b_bf16) + _dot(lo, b_bf16)


def _inproj_kernel(x_ref, g_ref, wtok_ref, wq_ref, wk_ref, wv_ref, tok_ref, q_ref, k_ref, v_ref):
    hb = _rms(x_ref[0], g_ref[...]).astype(BF16)
    tok_ref[0] = _dot(hb, wtok_ref[...])
    q_ref[0] = lax.dot_general(wq_ref[...], hb, _NT, preferred_element_type=F32)
    k_ref[0] = lax.dot_general(wk_ref[...], hb, _NT, preferred_element_type=F32)
    v_ref[0] = lax.dot_general(wv_ref[...], hb, _NT, preferred_element_type=F32)


def _inproj(x, g, wtok, wq_t, wk_t, wv_t, tm):
    nb, t, d = x.shape
    full = lambda s: pl.BlockSpec(s, lambda b, i: (0,) * len(s))
    chn = pl.BlockSpec((1, MOBA_W, tm), lambda b, i: (b, 0, i))
    return pl.pallas_call(
        _inproj_kernel,
        grid=(nb, t // tm),
        in_specs=[pl.BlockSpec((1, tm, d), lambda b, i: (b, i, 0)), full((1, d)), full((d, TOK_W)),
                  full((MOBA_W, d)), full((MOBA_W, d)), full((MOBA_W, d))],
        out_specs=[pl.BlockSpec((1, tm, TOK_W), lambda b, i: (b, i, 0)), chn, chn, chn],
        out_shape=[jax.ShapeDtypeStruct((nb, t, TOK_W), F32)] + [jax.ShapeDtypeStruct((nb, MOBA_W, t), F32)] * 3,
        compiler_params=_cparams("parallel", "parallel"),
        name="inproj",
    )(x, g, wtok, wq_t, wk_t, wv_t)


def _linrec_kernel(tok_ref, cos_ref, sin_ref, w2_ref, gb_ref, rdec_ref, rgain_ref, ggain_ref, s0_ref,
                   o_ref, sfin_ref, s_scr, *, tb, c, bpb):
    t = pl.program_id(1)

    @pl.when(t == 0)
    def _():
        s_scr[...] = s0_ref[...]

    n_sub = tb // c
    shift = int(math.log2(c))
    row = lax.broadcasted_iota(jnp.int32, (tb, tb), 0)
    col = lax.broadcasted_iota(jnp.int32, (tb, tb), 1)
    same = (row >> shift) == (col >> shift)
    tril = same & (col <= row)
    tril_b = jnp.where(tril, 1.0, 0.0).astype(BF16)
    same_b = jnp.where(same, 1.0, 0.0).astype(BF16)
    head_qk = lax.broadcasted_iota(jnp.int32, (1, LIN_QK_W), 1) // LIN_KEY_DIM
    head_v = lax.broadcasted_iota(jnp.int32, (1, LIN_V_W), 1) // HEAD_DIM
    blockdiag = (lax.broadcasted_iota(jnp.int32, (LIN_QK_W, LIN_V_W), 0) // LIN_KEY_DIM
                 == lax.broadcasted_iota(jnp.int32, (LIN_QK_W, LIN_V_W), 1) // HEAD_DIM)
    gi = lax.broadcasted_iota(jnp.int32, (LIN_V_W, LIN_V_W), 0) // HEAD_DIM
    gj = lax.broadcasted_iota(jnp.int32, (LIN_V_W, LIN_V_W), 1) // HEAD_DIM
    head_mean = jnp.where(gi == gj, 1.0 / HEAD_DIM, 0.0).astype(BF16)
    first_half = (lax.broadcasted_iota(jnp.int32, (1, LIN_QK_W), 1) % LIN_KEY_DIM) < LIN_KEY_DIM // 2
    cos = cos_ref[...]
    sin = sin_ref[...]
    qk_scale = LIN_KEY_DIM ** -0.5

    def rotary(x):
        rot = jnp.where(first_half, -pltpu.roll(x, LIN_QK_W - LIN_KEY_DIM // 2, 1), pltpu.roll(x, LIN_KEY_DIM // 2, 1))
        return x * cos + rot * sin

    sub_of_col = lax.broadcasted_iota(jnp.int32, (1, tb), 1) >> shift

    def group(bi, g_idx, q, k, v, b, b_end, gate, gain):
        q_dec = q * jnp.exp(b)
        k_inc = (k * jnp.exp(-b)).astype(BF16)
        k_dec = k * jnp.exp(b_end - b)
        b_end_t = b_end.T
        k_dec_t = k_dec.T
        vb = v.astype(BF16)
        q4 = jnp.concatenate([jnp.where(head_qk == h, q_dec, 0.0) for h in range(N_LIN_HEADS)], axis=0).astype(BF16)
        sc = lax.dot_general(q4, k_inc, _NT, preferred_element_type=F32)
        p4 = jnp.concatenate([jnp.where(tril, sc[h * tb:(h + 1) * tb], 0.0) for h in range(N_LIN_HEADS)], axis=0).astype(BF16)
        pv = _dot(p4, vb)
        o = jnp.where(head_v == 0, pv[0:tb], 0.0)
        for h in range(1, N_LIN_HEADS):
            o = o + jnp.where(head_v == h, pv[h * tb:(h + 1) * tb], 0.0)
        if n_sub == 1:
            u_all = lax.dot_general(k_dec.astype(BF16), vb, _TN, preferred_element_type=F32)
        else:
            k_stack = jnp.concatenate([jnp.where(sub_of_col == i, k_dec_t, 0.0) for i in range(n_sub)], axis=0)
            u_all = _dot(k_stack.astype(BF16), vb)
        s = s_scr[bi, g_idx]
        inter = []
        for i in range(n_sub):
            lo, hi = i * c, (i + 1) * c
            inter.append(_dot(q_dec[lo:hi].astype(BF16), s.astype(BF16)))
            u = u_all[i * LIN_QK_W:(i + 1) * LIN_QK_W]
            s = jnp.exp(b_end_t[:, lo:lo + 1]) * s + jnp.where(blockdiag, u, 0.0)
        s_scr[bi, g_idx] = s
        o = o + (inter[0] if n_sub == 1 else jnp.concatenate(inter, axis=0))
        ms = _rdot3(o * o, head_mean)
        on = o * lax.rsqrt(ms + RMS_EPS) * gain
        return on * (gate / (1.0 + jnp.exp(-gate)))

    a = LIN_QK_W
    w = LIN_V_W
    base = 2 * a + 2 * w
    for bi in range(bpb):
        rq = rotary(tok_ref[bi, :, 0:a])
        rk = rotary(tok_ref[bi, :, a:2 * a]) * qk_scale
        rv = tok_ref[bi, :, 2 * a:2 * a + w]
        rgate = tok_ref[bi, :, 2 * a + w:2 * a + 2 * w]
        pos_in_chunk = (lax.broadcasted_iota(jnp.int32, (tb, 1), 0) & (c - 1)).astype(F32)
        ret_b = (pos_in_chunk + 1.0) * rdec_ref[...]
        ret_b_end = jnp.broadcast_to(float(c) * rdec_ref[...], (tb, LIN_QK_W))
        o_ref[bi, :, 0:w] = group(bi, 0, rq, rk, rv, ret_b, ret_b_end, rgate, rgain_ref[...]).astype(o_ref.dtype)

        gq = tok_ref[bi, :, base:base + a] * qk_scale
        gk = tok_ref[bi, :, base + a:base + 2 * a]
        gv = tok_ref[bi, :, base + 2 * a:base + 2 * a + w]
        ggate = tok_ref[bi, :, base + 2 * a + w:base + 2 * a + 2 * w]
        ga = tok_ref[bi, :, LIN_TOK_W:TOK_W]
        z = _dot(ga.astype(BF16), w2_ref[...]) + gb_ref[...]
        gla_lg = (jnp.minimum(z, 0.0) - jnp.log(1.0 + jnp.exp(-jnp.abs(z)))) * (1.0 / GLA_TAU)
        gla_b = _ldot3(tril_b, gla_lg)
        gla_b_end = _ldot3(same_b, gla_lg)
        o_ref[bi, :, w:2 * w] = group(bi, 1, gq, gk, gv, gla_b, gla_b_end, ggate, ggain_ref[...]).astype(o_ref.dtype)

    @pl.when(t == pl.num_programs(1) - 1)
    def _():
        sfin_ref[...] = s_scr[...]


def _linrec(tok, cos, sin, w2p, gb, rdec, rgain, ggain, s0, tb, c):
    nb, t, _ = tok.shape
    bpb = 2 if nb % 2 == 0 else 1
    full = lambda s: pl.BlockSpec(s, lambda b, i: (0,) * len(s))
    st = pl.BlockSpec((bpb, 2, LIN_QK_W, LIN_V_W), lambda b, i: (b, 0, 0, 0))
    return pl.pallas_call(
        functools.partial(_linrec_kernel, tb=tb, c=c, bpb=bpb),
        grid=(nb // bpb, t // tb),
        in_specs=[pl.BlockSpec((bpb, tb, TOK_W), lambda b, i: (b, i, 0)),
                  pl.BlockSpec((tb, LIN_QK_W), lambda b, i: (i, 0)),
                  pl.BlockSpec((tb, LIN_QK_W), lambda b, i: (i, 0)),
                  full((GA_PAD, LIN_QK_W)), full((1, LIN_QK_W)), full((1, LIN_QK_W)),
                  full((1, LIN_V_W)), full((1, LIN_V_W)), st],
        out_specs=[pl.BlockSpec((bpb, tb, 2 * LIN_V_W), lambda b, i: (b, i, 0)), st],
        out_shape=[jax.ShapeDtypeStruct((nb, t, 2 * LIN_V_W), BF16),
                   jax.ShapeDtypeStruct((nb, 2, LIN_QK_W, LIN_V_W), F32)],
        scratch_shapes=[pltpu.VMEM((bpb, 2, LIN_QK_W, LIN_V_W), F32)],
        compiler_params=_cparams("parallel", "arbitrary"),
        name="linrec",
    )(tok, cos, sin, w2p, gb, rdec, rgain, ggain, s0)


def _near_bias_kernel(bias_ref, bucket_ref, tab_ref):
    head = pl.program_id(0)
    far_b = bias_ref[head * N_BUCKETS + N_BUCKETS - 1]
    for c0 in range(0, bucket_ref.shape[0], 128):
        bk = bucket_ref[c0:c0 + 128, :]

        def pick(bi, tab):
            return jnp.where(bk == bi, (bias_ref[head * N_BUCKETS + bi] - far_b) * LOG2E, tab)

        tab_ref[0, c0:c0 + 128, :] = lax.fori_loop(0, N_BUCKETS, pick, jnp.where(bk < 0, MASK_NEG, 0.0))


def _near_bias_table(bias_flat, bucket_near):
    rows, cols = bucket_near.shape
    return pl.pallas_call(
        _near_bias_kernel,
        grid_spec=pltpu.PrefetchScalarGridSpec(
            num_scalar_prefetch=1,
            grid=(H_MOBA,),
            in_specs=[pl.BlockSpec((rows, cols), lambda h, bias: (0, 0))],
            out_specs=pl.BlockSpec((1, rows, cols), lambda h, bias: (h, 0, 0)),
        ),
        out_shape=jax.ShapeDtypeStruct((H_MOBA, rows, cols), F32),
        compiler_params=_cparams("parallel"),
        name="near_bias",
    )(bias_flat, bucket_near)


def _select_topk(gate_t, n_valid, nbp, width):
    rown = lax.broadcasted_iota(jnp.int32, (nbp, width), 0)
    rown_f = rown.astype(F32)
    valid = rown < n_valid
    g = jnp.where(valid, gate_t, -jnp.inf)
    sel = jnp.zeros((nbp, width), F32)
    for _ in range(MOBA_TOPK):
        mx = jnp.max(g, axis=0, keepdims=True)
        idx = jnp.min(jnp.where(g == mx, rown_f, float(nbp)), axis=0, keepdims=True)
        pick = rown_f == idx
        sel = jnp.where(pick, jnp.where(valid, 1.0, 0.0), sel)
        g = jnp.where(pick, -jnp.inf, g)
    return sel, rown


def _moba_prompt_kernel(bias_ref, q_ref, k_ref, v_ref, near_ref, o_ref,
                        kaug_scr, vaug_scr, kmean_scr, p_scr, s2_scr, *, nb, nbp):
    hp = pl.program_id(1)
    qi = pl.program_id(2)
    blk = MOBA_BLOCK
    n_tail = HEAD_DIM - nbp
    c_far, c_dummy = nbp, nbp + 2

    @pl.when(qi == 0)
    def _():
        lane = lax.broadcasted_iota(jnp.int32, (blk, HEAD_DIM), 1)
        lane_aug = lax.broadcasted_iota(jnp.int32, (blk, 2 * HEAD_DIM), 1)
        ones_row = jnp.where(lax.broadcasted_iota(jnp.int32, (V_ROWS - HEAD_DIM, blk), 0) == 0, 1.0, 0.0).astype(BF16)
        for hh in range(2):
            rows = slice(hh * HEAD_DIM, (hh + 1) * HEAD_DIM)
            for d0 in (0, (nb + 1) * blk, (nb + 2) * blk):
                kaug_scr[hh, d0:d0 + blk, :] = jnp.where(lane_aug == HEAD_DIM + c_dummy, 1.0, 0.0).astype(BF16)
                vaug_scr[hh, 0:HEAD_DIM, d0:d0 + blk] = jnp.zeros((HEAD_DIM, blk), BF16)
                vaug_scr[hh, HEAD_DIM:V_ROWS, d0:d0 + blk] = ones_row
            kmean_scr[hh] = jnp.zeros((nbp, HEAD_DIM), F32)

            def fill(n, carry):
                off = pl.multiple_of(n * blk, blk)
                dst = pl.ds(pl.multiple_of((n + 1) * blk, blk), blk)
                kb = k_ref[0, rows, pl.ds(off, blk)].T
                kmean_scr[hh, pl.ds(n, 1), :] = jnp.mean(kb, axis=0, keepdims=True)
                extra = jnp.where((lane == n) | (lane == c_far) | (lane == c_far + 1), 1.0, 0.0)
                kaug_scr[hh, dst, :] = jnp.concatenate([kb, extra], axis=1).astype(BF16)
                vaug_scr[hh, 0:HEAD_DIM, dst] = v_ref[0, rows, pl.ds(off, blk)].astype(BF16)
                vaug_scr[hh, HEAD_DIM:V_ROWS, dst] = ones_row
                return carry

            lax.fori_loop(0, nb, fill, 0)

    tail_row = lax.broadcasted_iota(jnp.int32, (n_tail, blk), 0)
    near_off = pl.multiple_of(qi * blk, blk)
    q_near, q_far = [], []
    for hh in range(2):
        head = hp * 2 + hh
        q_t = q_ref[0, hh * HEAD_DIM:(hh + 1) * HEAD_DIM, :]
        gate_t = jnp.dot(kmean_scr[hh], q_t, preferred_element_type=F32, precision=lax.Precision.HIGHEST)
        sel, rown = _select_topk(gate_t, qi, nbp, blk)
        near_ok = (rown == qi) | ((rown == qi - 1) & (sel > 0.0))
        far_ok = (rown < qi - 1) & (sel > 0.0)
        far_v = jnp.full((n_tail, blk), bias_ref[head * N_BUCKETS + N_BUCKETS - 1] * LOG2E, F32)
        far_hi = far_v.astype(BF16).astype(F32)
        tail = jnp.where(tail_row == 0, far_hi, jnp.where(tail_row == 1, far_v - far_hi,
                                                          jnp.where(tail_row == 2, MASK_NEG, 0.0)))
        qs = q_t * (HEAD_DIM ** -0.5 * LOG2E)
        q_near.append(jnp.concatenate([qs, jnp.where(near_ok, 0.0, MASK_NEG), tail], axis=0).astype(BF16))
        q_far.append(jnp.concatenate([qs, jnp.where(far_ok, 0.0, MASK_NEG), tail], axis=0).astype(BF16))

    ck = 256
    n_ck = 2 * blk // ck
    n_it = (qi + 1) // 2

    def qk_far(t, buf):
        off = pl.multiple_of(t * 2 * blk, 2 * blk)
        maxes = []
        for hh in range(2):
            mloc = None
            for c in range(n_ck):
                sc = _dot(kaug_scr[hh, pl.ds(off + c * ck, ck), :], q_far[hh])
                s2_scr[buf, hh, c * ck:(c + 1) * ck, :] = sc
                cm = jnp.max(sc.reshape(ck // 8, 8, blk), axis=0)
                mloc = cm if mloc is None else jnp.maximum(mloc, cm)
            maxes.append(jnp.max(mloc, axis=0, keepdims=True))
        return maxes

    def soft_pv(v_off, buf, maxes, st):
        out = []
        for hh in range(2):
            m, acc = st[2 * hh], st[2 * hh + 1]
            m_new = jnp.maximum(m, maxes[hh])
            for c in range(n_ck):
                rows_c = slice(c * ck, (c + 1) * ck)
                p_scr[hh, rows_c, :] = jnp.exp2(s2_scr[buf, hh, rows_c, :] - m_new).astype(BF16)
            acc = acc * jnp.exp2(m - m_new) + _dot(vaug_scr[hh, :, pl.ds(v_off, 2 * blk)], p_scr[hh])
            out += [m_new, acc]
        return out

    near_max = []
    for hh in range(2):
        s = _dot(kaug_scr[hh, pl.ds(near_off, 2 * blk), :], q_near[hh]) + near_ref[hh]
        s2_scr[0, hh] = s
        near_max.append(jnp.max(s, axis=0, keepdims=True))

    def tile_pair(j, carry):
        st, max_a = list(carry[:4]), list(carry[4:])
        max_b = qk_far(2 * j, 1)
        off_a = pl.multiple_of(jnp.where(j == 0, near_off, (2 * j - 1) * 2 * blk), blk)
        st = soft_pv(off_a, 0, max_a, st)
        max_a = qk_far(jnp.minimum(2 * j + 1, n_it), 0)
        st = soft_pv(pl.multiple_of(2 * j * 2 * blk, 2 * blk), 1, max_b, st)
        return tuple(st + max_a)

    init = []
    for hh in range(2):
        init += [jnp.full((1, blk), -jnp.inf, F32), jnp.zeros((V_ROWS, blk), F32)]
    state = lax.fori_loop(0, (n_it + 2) // 2, tile_pair, tuple(init + near_max))[:4]
    for hh in range(2):
        acc = state[2 * hh + 1]
        o = acc[0:HEAD_DIM] / acc[HEAD_DIM:HEAD_DIM + 1]
        o_ref[0, :, hh * HEAD_DIM:(hh + 1) * HEAD_DIM] = o.T.astype(o_ref.dtype)


def _moba_prompt(q_t, k_t, v_t, bias_flat, near_tab):
    nbat, _, t = q_t.shape
    nb = t // MOBA_BLOCK
    nbp = max(8, -(-nb // 8) * 8)
    assert t % MOBA_BLOCK == 0 and nbp + 3 <= HEAD_DIM
    kv = pl.BlockSpec((1, 2 * HEAD_DIM, t), lambda b, hp, qi, bias: (b, hp, 0))
    return pl.pallas_call(
        functools.partial(_moba_prompt_kernel, nb=nb, nbp=nbp),
        grid_spec=pltpu.PrefetchScalarGridSpec(
            num_scalar_prefetch=1,
            grid=(nbat, H_MOBA // 2, nb),
            in_specs=[pl.BlockSpec((1, 2 * HEAD_DIM, MOBA_BLOCK), lambda b, hp, qi, bias: (b, hp, qi)), kv, kv,
                      pl.BlockSpec((2, 2 * MOBA_BLOCK, MOBA_BLOCK), lambda b, hp, qi, bias: (hp, 0, 0))],
            out_specs=pl.BlockSpec((1, MOBA_BLOCK, 2 * HEAD_DIM), lambda b, hp, qi, bias: (b, qi, hp)),
            scratch_shapes=[pltpu.VMEM((2, t + 3 * MOBA_BLOCK, 2 * HEAD_DIM), BF16),
                            pltpu.VMEM((2, V_ROWS, t + 3 * MOBA_BLOCK), BF16),
                            pltpu.VMEM((2, nbp, HEAD_DIM), F32),
                            pltpu.VMEM((2, 2 * MOBA_BLOCK, MOBA_BLOCK), BF16),
                            pltpu.VMEM((2, 2, 2 * MOBA_BLOCK, MOBA_BLOCK), F32)],
        ),
        out_shape=jax.ShapeDtypeStruct((nbat, t, MOBA_W), BF16),
        compiler_params=_cparams("parallel", "parallel", "arbitrary"),
        name="moba_prompt",
    )(bias_flat, q_t, k_t, v_t, near_tab)


N_PAGE_BUF = 48


def _sample_select_kernel(pt_ref, q_ref, cache_ref, sel_ref, buf, sem, km_scr, *, layer, n_pages, pages_per_block, n_buf):
    b = pl.program_id(0)
    nblk = n_pages // pages_per_block
    page = cache_ref.shape[-1]

    def copy(p, slot):
        return pltpu.make_async_copy(cache_ref.at[layer, pt_ref[b, p]], buf.at[slot], sem.at[slot])

    for p in range(n_buf):
        copy(p, p).start()
    km_scr[...] = jnp.zeros(km_scr.shape, F32)

    def block(j, carry):
        tot = None
        for u in range(pages_per_block):
            p = j * pages_per_block + u
            slot = p % n_buf
            copy(p, slot).wait()
            x = buf[slot].reshape(MOBA_W, page)
            tot = x if tot is None else tot + x

            @pl.when(p + n_buf < n_pages)
            def _():
                copy(p + n_buf, slot).start()

        km_scr[pl.ds(j, 1), :] = jnp.sum(tot.T, axis=0, keepdims=True) * (1.0 / (pages_per_block * page))
        return carry

    lax.fori_loop(0, nblk, block, 0)

    km = km_scr[...]
    nrow = km.shape[0]
    head_sum = (lax.broadcasted_iota(jnp.int32, (MOBA_W, 128), 0) // HEAD_DIM
                == lax.broadcasted_iota(jnp.int32, (MOBA_W, 128), 1))
    head_sum = jnp.where(head_sum, 1.0, 0.0).astype(BF16)
    rown = lax.broadcasted_iota(jnp.int32, (nrow, 128), 0)
    rown_f = rown.astype(F32)
    sel_ref[0] = jnp.zeros(sel_ref.shape[1:], jnp.int32)
    for t in range(q_ref.shape[1]):
        gate = _rdot3(km * q_ref[0, t:t + 1, :], head_sum)
        g = jnp.where(rown < nblk, gate, -jnp.inf)
        for r in range(MOBA_TOPK):
            mx = jnp.max(g, axis=0, keepdims=True)
            idx = jnp.min(jnp.where(g == mx, rown_f, float(nrow)), axis=0, keepdims=True)
            sel_ref[0, 4 * t + r:4 * t + r + 1, :] = idx.astype(jnp.int32)
            g = jnp.where(rown_f == idx, -jnp.inf, g)


def _sample_select(page_table, q_tb, cache_t, layer):
    nbat, n_pages = page_table.shape
    n_t = q_tb.shape[1]
    page = cache_t.shape[-1]
    ppb = MOBA_BLOCK // page
    nblk = n_pages // ppb
    assert n_pages % ppb == 0
    n_buf = min(N_PAGE_BUF, n_pages)
    sel_rows = -(-4 * n_t // 8) * 8
    return pl.pallas_call(
        functools.partial(_sample_select_kernel, layer=layer, n_pages=n_pages, pages_per_block=ppb, n_buf=n_buf),
        grid_spec=pltpu.PrefetchScalarGridSpec(
            num_scalar_prefetch=1,
            grid=(nbat,),
            in_specs=[pl.BlockSpec((1, n_t, MOBA_W), lambda b, pt: (b, 0, 0)),
                      pl.BlockSpec(memory_space=pl.ANY)],
            out_specs=pl.BlockSpec((1, sel_rows, 128), lambda b, pt: (b, 0, 0)),
            scratch_shapes=[pltpu.VMEM((n_buf, H_MOBA, HEAD_DIM, page), F32),
                            pltpu.SemaphoreType.DMA((n_buf,)),
                            pltpu.VMEM((-(-nblk // 8) * 8, MOBA_W), F32)],
        ),
        out_shape=jax.ShapeDtypeStruct((nbat, sel_rows, 128), jnp.int32),
        compiler_params=_cparams("arbitrary"),
        name="sample_select",
    )(page_table, q_tb, cache_t)


N_HEAD_BUF = 4
N_AHEAD = 3


def _sample_attend_kernel(pid_ref, sel_ref, q_ref, kn_ref, vn_ref, bias_ref, ck_ref, cv_ref, o_ref,
                          kbuf, vbuf, sem, *, layer, n_t, pages_per_block, nblk):
    b = pl.program_id(0)
    nbat = pl.num_programs(0)
    page = ck_ref.shape[-1]
    n_slot = n_t * MOBA_TOPK

    def copies(bb, h, par):
        out = []
        for slot in range(n_slot):
            for u in range(pages_per_block):
                pid = pid_ref[bb, (h * n_slot + slot) * pages_per_block + u]
                lanes = pl.ds(u * page, page)
                out.append(pltpu.make_async_copy(ck_ref.at[layer, pid, h], kbuf.at[par, slot, :, lanes], sem.at[par, 0]))
                out.append(pltpu.make_async_copy(cv_ref.at[layer, pid, h], vbuf.at[par, slot, :, lanes], sem.at[par, 1]))
        return out

    @pl.when(b == 0)
    def _():
        for h0 in range(N_AHEAD):
            for cp in copies(b, h0, h0):
                cp.start()

    causal_own = lax.broadcasted_iota(jnp.int32, (n_t, n_t), 1) <= lax.broadcasted_iota(jnp.int32, (n_t, n_t), 0)
    for h in range(H_MOBA):
        par = h % N_HEAD_BUF
        nxt = h + N_AHEAD
        if nxt < H_MOBA:
            for cp in copies(b, nxt, nxt % N_HEAD_BUF):
                cp.start()
        else:
            @pl.when(b + 1 < nbat)
            def _():
                for cp in copies(b + 1, nxt - H_MOBA, nxt % N_HEAD_BUF):
                    cp.start()

        for cp in copies(b, h, par):
            cp.wait()
        rows = slice(h * HEAD_DIM, (h + 1) * HEAD_DIM)
        qh = q_ref[0, rows, :] * (HEAD_DIM ** -0.5)
        kn = kn_ref[0, rows, :]
        vn = vn_ref[0, rows, :]
        s_rows, own_rows = [], []
        for t in range(n_t):
            qb = jnp.broadcast_to(qh[:, t:t + 1], (HEAD_DIM, MOBA_BLOCK))
            own_rows.append(jnp.sum(qh[:, t:t + 1] * kn, axis=0, keepdims=True) + bias_ref[h, n_t + 1 + t:n_t + 2 + t, 0:n_t])
            for r in range(MOBA_TOPK):
                slot = t * MOBA_TOPK + r
                j = sel_ref[b, h * n_slot + slot]
                prod = qb * kbuf[par, slot]
                s = jnp.sum(prod.reshape(HEAD_DIM // 8, 8, MOBA_BLOCK).sum(axis=0), axis=0, keepdims=True)
                s_rows.append(s + jnp.where(j == nblk - 1, bias_ref[h, t:t + 1, :], bias_ref[h, n_t:n_t + 1, :]))
        s_all = jnp.concatenate(s_rows, axis=0)
        s_own = jnp.where(causal_own, jnp.concatenate(own_rows, axis=0), MASK_NEG)
        m_slot = jnp.max(s_all, axis=1, keepdims=True)
        m_t = jnp.max(s_own, axis=1, keepdims=True)
        m_rows = []
        for t in range(n_t):
            mt = m_t[t:t + 1]
            for r in range(MOBA_TOPK):
                mt = jnp.maximum(mt, m_slot[t * MOBA_TOPK + r:t * MOBA_TOPK + r + 1])
            m_rows.append(mt)
        m_t = jnp.concatenate(m_rows, axis=0)
        m_rep = jnp.concatenate([m_rows[t] for t in range(n_t) for _ in range(MOBA_TOPK)], axis=0)
        p_all = jnp.exp(s_all - m_rep)
        p_own = jnp.exp(s_own - m_t)
        l_slot = jnp.sum(p_all, axis=1, keepdims=True)
        l_own = jnp.sum(p_own, axis=1, keepdims=True)
        for t in range(n_t):
            wsum = None
            l = l_own[t:t + 1]
            for r in range(MOBA_TOPK):
                slot = t * MOBA_TOPK + r
                l = l + l_slot[slot:slot + 1]
                term = p_all[slot:slot + 1] * vbuf[par, slot]
                wsum = term if wsum is None else wsum + term
            acc = jnp.sum(wsum, axis=1, keepdims=True) + jnp.sum(p_own[t:t + 1] * vn, axis=1, keepdims=True)
            o_ref[0, rows, t:t + 1] = acc / l


def _sample_attend(pids, sel, q_bt, kn_bt, vn_bt, bias_rows, cache_kt, cache_vt, layer, nblk):
    nbat = pids.shape[0]
    n_t = q_bt.shape[-1]
    page = cache_kt.shape[-1]
    ppb = MOBA_BLOCK // page
    n_slot = n_t * MOBA_TOPK
    new = pl.BlockSpec((1, MOBA_W, n_t), lambda b, pd, sl: (b, 0, 0))
    return pl.pallas_call(
        functools.partial(_sample_attend_kernel, layer=layer, n_t=n_t, pages_per_block=ppb, nblk=nblk),
        grid_spec=pltpu.PrefetchScalarGridSpec(
            num_scalar_prefetch=2,
            grid=(nbat,),
            in_specs=[new, new, new,
                      pl.BlockSpec(bias_rows.shape, lambda b, pd, sl: (0, 0, 0)),
                      pl.BlockSpec(memory_space=pl.ANY), pl.BlockSpec(memory_space=pl.ANY)],
            out_specs=new,
            scratch_shapes=[pltpu.VMEM((N_HEAD_BUF, n_slot, HEAD_DIM, MOBA_BLOCK), F32),
                            pltpu.VMEM((N_HEAD_BUF, n_slot, HEAD_DIM, MOBA_BLOCK), F32),
                            pltpu.SemaphoreType.DMA((N_HEAD_BUF, 2))],
        ),
        out_shape=jax.ShapeDtypeStruct((nbat, MOBA_W, n_t), F32),
        compiler_params=_cparams("arbitrary"),
        name="sample_attend",
    )(pids, sel, q_bt, kn_bt, vn_bt, bias_rows, cache_kt, cache_vt)


def _mlp_kernel(x_ref, olin_ref, omoba_ref, wl_ref, wm_ref, gpost_ref, fpre_ref, wg_ref, wu_ref, wd_ref, fpost_ref, y_ref):
    mix = _dot(olin_ref[...], wl_ref[...]) + _dot(omoba_ref[...], wm_ref[...])
    x1 = x_ref[...] + _rms(mix, gpost_ref[...])
    h = _rms(x1, fpre_ref[...]).astype(BF16)
    gate = _dot(h, wg_ref[...])
    up = _dot(h, wu_ref[...])
    act = (gate / (1.0 + jnp.exp(-gate)) * up).astype(BF16)
    y_ref[...] = x1 + _rms(_dot(act, wd_ref[...]), fpost_ref[...])


def _mlp(x, olin, omoba, w_lin, w_moba, gpost, fpre, wg, wu, wd, fpost, tm):
    n, d = x.shape
    const = lambda a: pl.BlockSpec(a.shape, lambda i: (0,) * a.ndim, pipeline_mode=pl.Buffered(1))
    rows = lambda w: pl.BlockSpec((tm, w), lambda i: (i, 0))
    return pl.pallas_call(
        _mlp_kernel,
        grid=(n // tm,),
        in_specs=[rows(d), rows(olin.shape[1]), rows(omoba.shape[1]), const(w_lin), const(w_moba), const(gpost),
                  const(fpre), const(wg), const(wu), const(wd), const(fpost)],
        out_specs=rows(d),
        out_shape=jax.ShapeDtypeStruct((n, d), F32),
        compiler_params=_cparams("parallel"),
        name="mlp",
    )(x, olin, omoba, w_lin, w_moba, gpost, fpre, wg, wu, wd, fpost)


def _t5_bucket_np(dist):
    n = np.maximum(dist, 0)
    max_exact = N_BUCKETS // 2
    nf = np.maximum(n, 1).astype(np.float32)
    large = max_exact + (np.log(nf / np.float32(max_exact)) / np.float32(math.log(MAX_DISTANCE / max_exact))
                         * np.float32(N_BUCKETS - max_exact)).astype(np.int32)
    large = np.minimum(large, N_BUCKETS - 1)
    return np.where(n < max_exact, n, large).astype(np.int32)


def _rotary_tables(pos):
    half = LIN_KEY_DIM // 2
    inv = 1.0 / (ROPE_BASE ** jnp.linspace(0.0, 1.0, half, dtype=F32))
    ang = pos.astype(F32)[:, None] * inv[None, :]
    tile = lambda a: jnp.tile(a, (1, 2 * N_LIN_HEADS))
    return tile(jnp.cos(ang)), tile(jnp.sin(ang))


def _blockdiag_state(s):
    eye = jnp.eye(N_LIN_HEADS, dtype=s.dtype)
    return (s[:, :, :, None, :] * eye[None, :, None, :, None]).reshape(s.shape[0], LIN_QK_W, LIN_V_W)


def _head_states(sb):
    x = sb.reshape(sb.shape[0], N_LIN_HEADS, LIN_KEY_DIM, N_LIN_HEADS, HEAD_DIM)
    eye = jnp.eye(N_LIN_HEADS, dtype=sb.dtype)
    return jnp.sum(x * eye[None, :, None, :, None], axis=3)


def _row_tile(n, want):
    t = min(n, want)
    while n % t:
        t //= 2
    return t


def kernel(x_prompt, x_sample, cache_k, cache_v, state_ret, state_gla, page_table, w_in, w_out, gla_gate_w2, gla_gate_b, ret_norm_g, gla_norm_g, norm_attn_pre, norm_attn_post, norm_ffn_pre, norm_ffn_post, w_ffn_gate, w_ffn_up, w_ffn_down, rel_bias):
    depth = w_in.shape[0]
    nbp_, t_p, d_model = x_prompt.shape
    nbs, t_s, _ = x_sample.shape
    n_pages = page_table.shape[1]
    page = cache_k.shape[2]
    past_len = n_pages * page
    assert past_len % MOBA_BLOCK == 0 and t_s <= MOBA_BLOCK and t_p % MOBA_BLOCK == 0
    dff = w_ffn_gate.shape[2]

    cos_p, sin_p = _rotary_tables(jnp.arange(t_p, dtype=jnp.int32))
    cos_s, sin_s = _rotary_tables(past_len + jnp.arange(t_s, dtype=jnp.int32))
    off = np.arange(MOBA_BLOCK)
    dist_own = off[None, :] - off[:, None]
    bucket_near = np.concatenate([_t5_bucket_np(MOBA_BLOCK + dist_own),
                                  np.where(dist_own >= 0, _t5_bucket_np(dist_own), -1)], axis=0).astype(np.int32)
    bias_h = rel_bias.astype(F32).T
    bias_flat = bias_h.reshape(-1)
    near_tab = _near_bias_table(bias_flat, bucket_near)
    far = bias_h[:, N_BUCKETS - 1]
    prev_rows = bias_h[:, _t5_bucket_np(MOBA_BLOCK + np.arange(t_s)[:, None] - off[None, :])]
    far_row = jnp.broadcast_to(far[:, None, None], (H_MOBA, 1, MOBA_BLOCK))
    own_rows = bias_h[:, _t5_bucket_np(np.arange(t_s)[:, None] - off[None, :])]
    n_rows = 2 * t_s + 1
    pad_rows = -(-n_rows // 8) * 8 - n_rows
    bias_rows = jnp.concatenate([prev_rows, far_row, own_rows, jnp.zeros((H_MOBA, pad_rows, MOBA_BLOCK), F32)], axis=1)

    rdec = jnp.repeat(jnp.log(1.0 - 2.0 ** (-5.0 - jnp.arange(N_LIN_HEADS, dtype=F32))), LIN_KEY_DIM)[None, :]
    cache_kt = jnp.transpose(cache_k, (0, 1, 3, 4, 2))
    cache_vt = jnp.transpose(cache_v, (0, 1, 3, 4, 2))
    w_in_t = jnp.swapaxes(w_in, 1, 2)

    hp = x_prompt
    hs = x_sample.reshape(1, nbs * t_s, d_model)
    zero_state = jnp.zeros((nbp_, 2, LIN_QK_W, LIN_V_W), F32)
    tm_p = _row_tile(t_p, 512)
    tb_p = _row_tile(t_p, 256)
    c_p = math.gcd(t_p, REC_CHUNK)
    c_s = math.gcd(t_s, REC_CHUNK)
    outs = {k: [] for k in ("kp", "vp", "rp", "gp", "ks", "vs", "rs", "gs")}

    for l in range(depth):
        seg = w_in_t[l]
        wtok = jnp.concatenate([seg[:LIN_TOK_W + GLA_RANK], jnp.zeros((GA_PAD - GLA_RANK, d_model), F32)], axis=0).T.astype(BF16)
        m0 = LIN_TOK_W + GLA_RANK
        wq_t = seg[m0:m0 + MOBA_W].astype(BF16)
        wk_t = seg[m0 + MOBA_W:m0 + 2 * MOBA_W].astype(BF16)
        wv_t = seg[m0 + 2 * MOBA_W:m0 + 3 * MOBA_W].astype(BF16)
        w2p = jnp.concatenate([gla_gate_w2[l], jnp.zeros((GA_PAD - GLA_RANK, LIN_QK_W), F32)], axis=0).astype(BF16)
        gb = gla_gate_b[l][None, :]
        rgain = jnp.tile(ret_norm_g[l], N_LIN_HEADS)[None, :]
        ggain = jnp.tile(gla_norm_g[l], N_LIN_HEADS)[None, :]
        w_o = w_out[l].astype(BF16)
        w_o_lin, w_o_moba = w_o[:2 * LIN_V_W], w_o[2 * LIN_V_W:]
        wg, wu, wd = w_ffn_gate[l].astype(BF16), w_ffn_up[l].astype(BF16), w_ffn_down[l].astype(BF16)
        g_pre, g_post = norm_attn_pre[l][None, :], norm_attn_post[l][None, :]
        f_pre, f_post = norm_ffn_pre[l][None, :], norm_ffn_post[l][None, :]

        tok, q_t, k_t, v_t = _inproj(hp, g_pre, wtok, wq_t, wk_t, wv_t, tm_p)
        olin, sfin = _linrec(tok, cos_p, sin_p, w2p, gb, rdec, rgain, ggain, zero_state, tb_p, c_p)
        omoba = _moba_prompt(q_t, k_t, v_t, bias_flat, near_tab)
        hp = _mlp(hp.reshape(-1, d_model), olin.reshape(-1, 2 * LIN_V_W), omoba.reshape(-1, MOBA_W),
                  w_o_lin, w_o_moba, g_post, f_pre, wg, wu, wd, f_post, tm_p).reshape(nbp_, t_p, d_model)
        outs["kp"].append(jnp.transpose(k_t.reshape(nbp_, H_MOBA, HEAD_DIM, t_p), (0, 3, 1, 2)))
        outs["vp"].append(jnp.transpose(v_t.reshape(nbp_, H_MOBA, HEAD_DIM, t_p), (0, 3, 1, 2)))
        outs["rp"].append(_head_states(sfin[:, 0]))
        outs["gp"].append(_head_states(sfin[:, 1]))

        n_s = nbs * t_s
        tok, q_t, k_t, v_t = _inproj(hs, g_pre, wtok, wq_t, wk_t, wv_t, n_s)
        s0 = jnp.stack([_blockdiag_state(state_ret[l]), _blockdiag_state(state_gla[l])], axis=1)
        olin, sfin = _linrec(tok.reshape(nbs, t_s, TOK_W), cos_s, sin_s, w2p, gb, rdec, rgain, ggain, s0, t_s, c_s)
        per_batch = lambda a: jnp.transpose(a.reshape(MOBA_W, nbs, t_s), (1, 0, 2))
        q_bt, k_bt, v_bt = per_batch(q_t), per_batch(k_t), per_batch(v_t)
        sel = _sample_select(page_table, jnp.transpose(q_bt, (0, 2, 1)), cache_kt, l)
        sel = jnp.transpose(sel[:, :4 * t_s, :H_MOBA].reshape(nbs, t_s, 4, H_MOBA)[:, :, :MOBA_TOPK], (0, 3, 1, 2))
        sel = sel.reshape(nbs, -1)
        ppb = MOBA_BLOCK // page
        logical = (sel[:, :, None] * ppb + jnp.arange(ppb, dtype=jnp.int32)[None, None, :]).reshape(nbs, -1)
        hit = logical[:, :, None] == jnp.arange(n_pages, dtype=jnp.int32)[None, None, :]
        pids = jnp.sum(jnp.where(hit, page_table[:, None, :], 0), axis=-1)
        o_bt = _sample_attend(pids, sel, q_bt, k_bt, v_bt, bias_rows, cache_kt, cache_vt, l, n_pages // ppb)
        omoba = jnp.transpose(o_bt, (0, 2, 1)).reshape(n_s, MOBA_W).astype(BF16)
        hs = _mlp(hs.reshape(n_s, d_model), olin.reshape(n_s, 2 * LIN_V_W), omoba, w_o_lin, w_o_moba, g_post,
                  f_pre, wg, wu, wd, f_post, n_s).reshape(1, n_s, d_model)
        to_cache = lambda a: jnp.transpose(a.reshape(H_MOBA, HEAD_DIM, nbs, t_s), (2, 3, 0, 1))
        outs["ks"].append(to_cache(k_t))
        outs["vs"].append(to_cache(v_t))
        outs["rs"].append(_head_states(sfin[:, 0]))
        outs["gs"].append(_head_states(sfin[:, 1]))

    st = lambda k: jnp.stack(outs[k])
    return (hp, hs.reshape(nbs, t_s, d_model), st("kp"), st("vp"), st("rp"), st("gp"),
            st("ks"), st("vs"), st("rs"), st("gs"))
```

```python
import functools
import math

import jax
import jax.numpy as jnp
import numpy as np
from jax import lax
from jax.experimental import pallas as pl
from jax.experimental.pallas import tpu as pltpu

F32 = jnp.float32
BF16 = jnp.bfloat16

HEAD_DIM = 64
LIN_KEY_DIM = 32
N_LIN_HEADS = 4
H_MOBA = 8
LIN_QK_W = N_LIN_HEADS * LIN_KEY_DIM
LIN_V_W = N_LIN_HEADS * HEAD_DIM
MOBA_W = H_MOBA * HEAD_DIM
GLA_RANK = 16
GLA_TAU = 16.0
REC_CHUNK = 16
ROPE_BASE = 10000.0
MOBA_BLOCK = 256
MOBA_TOPK = 3
N_BUCKETS = 32
MAX_DISTANCE = 128
RMS_EPS = 1e-6
LIN_TOK_W = 2 * (2 * LIN_QK_W + 2 * LIN_V_W)
GA_PAD = 128
TOK_W = LIN_TOK_W + GA_PAD

VMEM_LIMIT_BYTES = 56 * 1024 * 1024
MASK_NEG = -1e30
LOG2E = 1.4426950408889634
V_ROWS = 80

_NT = (((1,), (1,)), ((), ()))
_TN = (((0,), (0,)), ((), ()))


def _cparams(*sem):
    return pltpu.CompilerParams(dimension_semantics=sem, vmem_limit_bytes=VMEM_LIMIT_BYTES)


def _rms(x, g):
    return x * lax.rsqrt(jnp.mean(x * x, axis=-1, keepdims=True) + RMS_EPS) * g


def _split3(x):
    hi = x.astype(BF16)
    r = x - hi.astype(F32)
    mid = r.astype(BF16)
    lo = (r - mid.astype(F32)).astype(BF16)
    return hi, mid, lo


def _dot(a, b):
    return jnp.dot(a, b, preferred_element_type=F32)


def _ldot3(a_bf16, x):
    hi, mid, lo = _split3(x)
    return _dot(a_bf16, hi) + _dot(a_bf16, mid) + _dot(a_bf16, lo)


def _rdot3(x, b_bf16):
    hi, mid, lo = _split3(x)
    return _dot(hi, b_bf16) + _dot(mid, b_bf16) + _dot(lo, b_bf16)


def _inproj_kernel(x_ref, g_ref, wtok_ref, wq_ref, wk_ref, wv_ref, tok_ref, q_ref, k_ref, v_ref):
    hb = _rms(x_ref[0], g_ref[...]).astype(BF16)
    tok_ref[0] = _dot(hb, wtok_ref[...])
    q_ref[0] = lax.dot_general(wq_ref[...], hb, _NT, preferred_element_type=F32)
    k_ref[0] = lax.dot_general(wk_ref[...], hb, _NT, preferred_element_type=F32)
    v_ref[0] = lax.dot_general(wv_ref[...], hb, _NT, preferred_element_type=F32)


def _inproj(x, g, wtok, wq_t, wk_t, wv_t, tm):
    nb, t, d = x.shape
    full = lambda s: pl.BlockSpec(s, lambda b, i: (0,) * len(s))
    chn = pl.BlockSpec((1, MOBA_W, tm), lambda b, i: (b, 0, i))
    return pl.pallas_call(
        _inproj_kernel,
        grid=(nb, t // tm),
        in_specs=[pl.BlockSpec((1, tm, d), lambda b, i: (b, i, 0)), full((1, d)), full((d, TOK_W)),
                  full((MOBA_W, d)), full((MOBA_W, d)), full((MOBA_W, d))],
        out_specs=[pl.BlockSpec((1, tm, TOK_W), lambda b, i: (b, i, 0)), chn, chn, chn],
        out_shape=[jax.ShapeDtypeStruct((nb, t, TOK_W), F32)] + [jax.ShapeDtypeStruct((nb, MOBA_W, t), F32)] * 3,
        compiler_params=_cparams("parallel", "parallel"),
        name="inproj",
    )(x, g, wtok, wq_t, wk_t, wv_t)


def _linrec_kernel(tok_ref, cos_ref, sin_ref, w2_ref, gb_ref, rdec_ref, rgain_ref, ggain_ref, s0_ref,
                   o_ref, sfin_ref, s_scr, *, tb, c, bpb):
    t = pl.program_id(1)

    @pl.when(t == 0)
    def _():
        s_scr[...] = s0_ref[...]

    n_sub = tb // c
    shift = int(math.log2(c))
    row = lax.broadcasted_iota(jnp.int32, (tb, tb), 0)
    col = lax.broadcasted_iota(jnp.int32, (tb, tb), 1)
    same = (row >> shift) == (col >> shift)
    tril = same & (col <= row)
    tril_b = jnp.where(tril, 1.0, 0.0).astype(BF16)
    same_b = jnp.where(same, 1.0, 0.0).astype(BF16)
    head_qk = lax.broadcasted_iota(jnp.int32, (1, LIN_QK_W), 1) // LIN_KEY_DIM
    head_v = lax.broadcasted_iota(jnp.int32, (1, LIN_V_W), 1) // HEAD_DIM
    blockdiag = (lax.broadcasted_iota(jnp.int32, (LIN_QK_W, LIN_V_W), 0) // LIN_KEY_DIM
                 == lax.broadcasted_iota(jnp.int32, (LIN_QK_W, LIN_V_W), 1) // HEAD_DIM)
    gi = lax.broadcasted_iota(jnp.int32, (LIN_V_W, LIN_V_W), 0) // HEAD_DIM
    gj = lax.broadcasted_iota(jnp.int32, (LIN_V_W, LIN_V_W), 1) // HEAD_DIM
    head_mean = jnp.where(gi == gj, 1.0 / HEAD_DIM, 0.0).astype(BF16)
    first_half = (lax.broadcasted_iota(jnp.int32, (1, LIN_QK_W), 1) % LIN_KEY_DIM) < LIN_KEY_DIM // 2
    cos = cos_ref[...]
    sin = sin_ref[...]
    qk_scale = LIN_KEY_DIM ** -0.5

    def rotary(x):
        rot = jnp.where(first_half, -pltpu.roll(x, LIN_QK_W - LIN_KEY_DIM // 2, 1), pltpu.roll(x, LIN_KEY_DIM // 2, 1))
        return x * cos + rot * sin

    sub_of_col = lax.broadcasted_iota(jnp.int32, (1, tb), 1) >> shift

    def group(bi, g_idx, q, k, v, b, b_end, gate, gain):
        q_dec = q * jnp.exp(b)
        k_inc = (k * jnp.exp(-b)).astype(BF16)
        k_dec = k * jnp.exp(b_end - b)
        b_end_t = b_end.T
        k_dec_t = k_dec.T
        vb = v.astype(BF16)
        q4 = jnp.concatenate([jnp.where(head_qk == h, q_dec, 0.0) for h in range(N_LIN_HEADS)], axis=0).astype(BF16)
        sc = lax.dot_general(q4, k_inc, _NT, preferred_element_type=F32)
        p4 = jnp.concatenate([jnp.where(tril, sc[h * tb:(h + 1) * tb], 0.0) for h in range(N_LIN_HEADS)], axis=0).astype(BF16)
        pv = _dot(p4, vb)
        o = jnp.where(head_v == 0, pv[0:tb], 0.0)
        for h in range(1, N_LIN_HEADS):
            o = o + jnp.where(head_v == h, pv[h * tb:(h + 1) * tb], 0.0)
        if n_sub == 1:
            u_all = lax.dot_general(k_dec.astype(BF16), vb, _TN, preferred_element_type=F32)
        else:
            k_stack = jnp.concatenate([jnp.where(sub_of_col == i, k_dec_t, 0.0) for i in range(n_sub)], axis=0)
            u_all = _dot(k_stack.astype(BF16), vb)
        s = s_scr[bi, g_idx]
        inter = []
        for i in range(n_sub):
            lo, hi = i * c, (i + 1) * c
            inter.append(_dot(q_dec[lo:hi].astype(BF16), s.astype(BF16)))
            u = u_all[i * LIN_QK_W:(i + 1) * LIN_QK_W]
            s = jnp.exp(b_end_t[:, lo:lo + 1]) * s + jnp.where(blockdiag, u, 0.0)
        s_scr[bi, g_idx] = s
        o = o + (inter[0] if n_sub == 1 else jnp.concatenate(inter, axis=0))
        ms = _rdot3(o * o, head_mean)
        on = o * lax.rsqrt(ms + RMS_EPS) * gain
        return on * (gate / (1.0 + jnp.exp(-gate)))

    a = LIN_QK_W
    w = LIN_V_W
    base = 2 * a + 2 * w
    for bi in range(bpb):
        rq = rotary(tok_ref[bi, :, 0:a])
        rk = rotary(tok_ref[bi, :, a:2 * a]) * qk_scale
        rv = tok_ref[bi, :, 2 * a:2 * a + w]
        rgate = tok_ref[bi, :, 2 * a + w:2 * a + 2 * w]
        pos_in_chunk = (lax.broadcasted_iota(jnp.int32, (tb, 1), 0) & (c - 1)).astype(F32)
        ret_b = (pos_in_chunk + 1.0) * rdec_ref[...]
        ret_b_end = jnp.broadcast_to(float(c) * rdec_ref[...], (tb, LIN_QK_W))
        o_ref[bi, :, 0:w] = group(bi, 0, rq, rk, rv, ret_b, ret_b_end, rgate, rgain_ref[...]).astype(o_ref.dtype)

        gq = tok_ref[bi, :, base:base + a] * qk_scale
        gk = tok_ref[bi, :, base + a:base + 2 * a]
        gv = tok_ref[bi, :, base + 2 * a:base + 2 * a + w]
        ggate = tok_ref[bi, :, base + 2 * a + w:base + 2 * a + 2 * w]
        ga = tok_ref[bi, :, LIN_TOK_W:TOK_W]
        z = _dot(ga.astype(BF16), w2_ref[...]) + gb_ref[...]
        gla_lg = (jnp.minimum(z, 0.0) - jnp.log(1.0 + jnp.exp(-jnp.abs(z)))) * (1.0 / GLA_TAU)
        gla_b = _ldot3(tril_b, gla_lg)
        gla_b_end = _ldot3(same_b, gla_lg)
        o_ref[bi, :, w:2 * w] = group(bi, 1, gq, gk, gv, gla_b, gla_b_end, ggate, ggain_ref[...]).astype(o_ref.dtype)

    @pl.when(t == pl.num_programs(1) - 1)
    def _():
        sfin_ref[...] = s_scr[...]


def _linrec(tok, cos, sin, w2p, gb, rdec, rgain, ggain, s0, tb, c):
    nb, t, _ = tok.shape
    bpb = 2 if nb % 2 == 0 else 1
    full = lambda s: pl.BlockSpec(s, lambda b, i: (0,) * len(s))
    st = pl.BlockSpec((bpb, 2, LIN_QK_W, LIN_V_W), lambda b, i: (b, 0, 0, 0))
    return pl.pallas_call(
        functools.partial(_linrec_kernel, tb=tb, c=c, bpb=bpb),
        grid=(nb // bpb, t // tb),
        in_specs=[pl.BlockSpec((bpb, tb, TOK_W), lambda b, i: (b, i, 0)),
                  pl.BlockSpec((tb, LIN_QK_W), lambda b, i: (i, 0)),
                  pl.BlockSpec((tb, LIN_QK_W), lambda b, i: (i, 0)),
                  full((GA_PAD, LIN_QK_W)), full((1, LIN_QK_W)), full((1, LIN_QK_W)),
                  full((1, LIN_V_W)), full((1, LIN_V_W)), st],
        out_specs=[pl.BlockSpec((bpb, tb, 2 * LIN_V_W), lambda b, i: (b, i, 0)), st],
        out_shape=[jax.ShapeDtypeStruct((nb, t, 2 * LIN_V_W), BF16),
                   jax.ShapeDtypeStruct((nb, 2, LIN_QK_W, LIN_V_W), F32)],
        scratch_shapes=[pltpu.VMEM((bpb, 2, LIN_QK_W, LIN_V_W), F32)],
        compiler_params=_cparams("parallel", "arbitrary"),
        name="linrec",
    )(tok, cos, sin, w2p, gb, rdec, rgain, ggain, s0)


def _near_bias_kernel(bias_ref, bucket_ref, tab_ref):
    head = pl.program_id(0)
    far_b = bias_ref[head * N_BUCKETS + N_BUCKETS - 1]
    for c0 in range(0, bucket_ref.shape[0], 128):
        bk = bucket_ref[c0:c0 + 128, :]

        def pick(bi, tab):
            return jnp.where(bk == bi, (bias_ref[head * N_BUCKETS + bi] - far_b) * LOG2E, tab)

        tab_ref[0, c0:c0 + 128, :] = lax.fori_loop(0, N_BUCKETS, pick, jnp.where(bk < 0, MASK_NEG, 0.0))


def _near_bias_table(bias_flat, bucket_near):
    rows, cols = bucket_near.shape
    return pl.pallas_call(
        _near_bias_kernel,
        grid_spec=pltpu.PrefetchScalarGridSpec(
            num_scalar_prefetch=1,
            grid=(H_MOBA,),
            in_specs=[pl.BlockSpec((rows, cols), lambda h, bias: (0, 0))],
            out_specs=pl.BlockSpec((1, rows, cols), lambda h, bias: (h, 0, 0)),
        ),
        out_shape=jax.ShapeDtypeStruct((H_MOBA, rows, cols), F32),
        compiler_params=_cparams("parallel"),
        name="near_bias",
    )(bias_flat, bucket_near)


def _select_topk(gate_t, n_valid, nbp, width):
    rown = lax.broadcasted_iota(jnp.int32, (nbp, width), 0)
    rown_f = rown.astype(F32)
    valid = rown < n_valid
    g = jnp.where(valid, gate_t, -jnp.inf)
    sel = jnp.zeros((nbp, width), F32)
    for _ in range(MOBA_TOPK):
        mx = jnp.max(g, axis=0, keepdims=True)
        idx = jnp.min(jnp.where(g == mx, rown_f, float(nbp)), axis=0, keepdims=True)
        pick = rown_f == idx
        sel = jnp.where(pick, jnp.where(valid, 1.0, 0.0), sel)
        g = jnp.where(pick, -jnp.inf, g)
    return sel, rown


def _moba_prompt_kernel(bias_ref, q_ref, k_ref, v_ref, near_ref, o_ref,
                        kaug_scr, vaug_scr, kmean_scr, p_scr, s2_scr, *, nb, nbp, qps):
    hp = pl.program_id(1)
    step = pl.program_id(2)
    blk = MOBA_BLOCK
    n_tail = HEAD_DIM - nbp
    c_far, c_dummy = nbp, nbp + 2

    @pl.when(step == 0)
    def _():
        lane = lax.broadcasted_iota(jnp.int32, (blk, HEAD_DIM), 1)
        lane_aug = lax.broadcasted_iota(jnp.int32, (blk, 2 * HEAD_DIM), 1)
        ones_row = jnp.where(lax.broadcasted_iota(jnp.int32, (V_ROWS - HEAD_DIM, blk), 0) == 0, 1.0, 0.0).astype(BF16)
        for hh in range(2):
            rows = slice(hh * HEAD_DIM, (hh + 1) * HEAD_DIM)
            for d0 in (0, (nb + 1) * blk, (nb + 2) * blk):
                kaug_scr[hh, d0:d0 + blk, :] = jnp.where(lane_aug == HEAD_DIM + c_dummy, 1.0, 0.0).astype(BF16)
                vaug_scr[hh, 0:HEAD_DIM, d0:d0 + blk] = jnp.zeros((HEAD_DIM, blk), BF16)
                vaug_scr[hh, HEAD_DIM:V_ROWS, d0:d0 + blk] = ones_row
            kmean_scr[hh] = jnp.zeros((nbp, HEAD_DIM), F32)

            def fill(n, carry):
                off = pl.multiple_of(n * blk, blk)
                dst = pl.ds(pl.multiple_of((n + 1) * blk, blk), blk)
                kb = k_ref[0, rows, pl.ds(off, blk)].T
                kmean_scr[hh, pl.ds(n, 1), :] = jnp.mean(kb, axis=0, keepdims=True)
                extra = jnp.where((lane == n) | (lane == c_far) | (lane == c_far + 1), 1.0, 0.0)
                kaug_scr[hh, dst, :] = jnp.concatenate([kb, extra], axis=1).astype(BF16)
                vaug_scr[hh, 0:HEAD_DIM, dst] = v_ref[0, rows, pl.ds(off, blk)].astype(BF16)
                vaug_scr[hh, HEAD_DIM:V_ROWS, dst] = ones_row
                return carry

            lax.fori_loop(0, nb, fill, 0)

    tail_row = lax.broadcasted_iota(jnp.int32, (n_tail, blk), 0)
    for u in range(qps):
        qi = step * qps + u
        near_off = pl.multiple_of(qi * blk, blk)
        q_near, q_far = [], []
        for hh in range(2):
            head = hp * 2 + hh
            q_t = q_ref[0, hh * HEAD_DIM:(hh + 1) * HEAD_DIM, u * blk:(u + 1) * blk]
            gate_t = jnp.dot(kmean_scr[hh], q_t, preferred_element_type=F32, precision=lax.Precision.HIGHEST)
            sel, rown = _select_topk(gate_t, qi, nbp, blk)
            near_ok = (rown == qi) | ((rown == qi - 1) & (sel > 0.0))
            far_ok = (rown < qi - 1) & (sel > 0.0)
            far_v = jnp.full((n_tail, blk), bias_ref[head * N_BUCKETS + N_BUCKETS - 1] * LOG2E, F32)
            far_hi = far_v.astype(BF16).astype(F32)
            tail = jnp.where(tail_row == 0, far_hi, jnp.where(tail_row == 1, far_v - far_hi,
                                                              jnp.where(tail_row == 2, MASK_NEG, 0.0)))
            qs = q_t * (HEAD_DIM ** -0.5 * LOG2E)
            q_near.append(jnp.concatenate([qs, jnp.where(near_ok, 0.0, MASK_NEG), tail], axis=0).astype(BF16))
            q_far.append(jnp.concatenate([qs, jnp.where(far_ok, 0.0, MASK_NEG), tail], axis=0).astype(BF16))

        ck = 256
        n_ck = 2 * blk // ck
        n_it = (qi + 1) // 2

        def qk_far(t, buf):
            off = pl.multiple_of(t * 2 * blk, 2 * blk)
            maxes = []
            for hh in range(2):
                mloc = None
                for c in range(n_ck):
                    sc = _dot(kaug_scr[hh, pl.ds(off + c * ck, ck), :], q_far[hh])
                    s2_scr[buf, hh, c * ck:(c + 1) * ck, :] = sc
                    cm = jnp.max(sc.reshape(ck // 8, 8, blk), axis=0)
                    mloc = cm if mloc is None else jnp.maximum(mloc, cm)
                maxes.append(jnp.max(mloc, axis=0, keepdims=True))
            return maxes

        def soft_pv(v_off, buf, maxes, st):
            out = []
            for hh in range(2):
                m, acc = st[2 * hh], st[2 * hh + 1]
                m_new = jnp.maximum(m, maxes[hh])
                for c in range(n_ck):
                    rows_c = slice(c * ck, (c + 1) * ck)
                    p_scr[hh, rows_c, :] = jnp.exp2(s2_scr[buf, hh, rows_c, :] - m_new).astype(BF16)
                acc = acc * jnp.exp2(m - m_new) + _dot(vaug_scr[hh, :, pl.ds(v_off, 2 * blk)], p_scr[hh])
                out += [m_new, acc]
            return out

        near_max = []
        for hh in range(2):
            s = _dot(kaug_scr[hh, pl.ds(near_off, 2 * blk), :], q_near[hh]) + near_ref[hh]
            s2_scr[0, hh] = s
            near_max.append(jnp.max(s, axis=0, keepdims=True))

        def tile_pair(j, carry):
            st, max_a = list(carry[:4]), list(carry[4:])
            max_b = qk_far(2 * j, 1)
            off_a = pl.multiple_of(jnp.where(j == 0, near_off, (2 * j - 1) * 2 * blk), blk)
            st = soft_pv(off_a, 0, max_a, st)
            max_a = qk_far(jnp.minimum(2 * j + 1, n_it), 0)
            st = soft_pv(pl.multiple_of(2 * j * 2 * blk, 2 * blk), 1, max_b, st)
            return tuple(st + max_a)

        init = []
        for hh in range(2):
            init += [jnp.full((1, blk), -jnp.inf, F32), jnp.zeros((V_ROWS, blk), F32)]
        state = lax.fori_loop(0, (n_it + 2) // 2, tile_pair, tuple(init + near_max))[:4]
        for hh in range(2):
            acc = state[2 * hh + 1]
            o = acc[0:HEAD_DIM] / acc[HEAD_DIM:HEAD_DIM + 1]
            o_ref[0, u * blk:(u + 1) * blk, hh * HEAD_DIM:(hh + 1) * HEAD_DIM] = o.T.astype(o_ref.dtype)


def _moba_prompt(q_t, k_t, v_t, bias_flat, near_tab):
    nbat, _, t = q_t.shape
    nb = t // MOBA_BLOCK
    nbp = max(8, -(-nb // 8) * 8)
    assert t % MOBA_BLOCK == 0 and nbp + 3 <= HEAD_DIM
    qps = 4 if nb % 4 == 0 else (2 if nb % 2 == 0 else 1)
    kv = pl.BlockSpec((1, 2 * HEAD_DIM, t), lambda b, hp, qi, bias: (b, hp, 0))
    return pl.pallas_call(
        functools.partial(_moba_prompt_kernel, nb=nb, nbp=nbp, qps=qps),
        grid_spec=pltpu.PrefetchScalarGridSpec(
            num_scalar_prefetch=1,
            grid=(nbat, H_MOBA // 2, nb // qps),
            in_specs=[pl.BlockSpec((1, 2 * HEAD_DIM, qps * MOBA_BLOCK), lambda b, hp, qi, bias: (b, hp, qi)), kv, kv,
                      pl.BlockSpec((2, 2 * MOBA_BLOCK, MOBA_BLOCK), lambda b, hp, qi, bias: (hp, 0, 0))],
            out_specs=pl.BlockSpec((1, qps * MOBA_BLOCK, 2 * HEAD_DIM), lambda b, hp, qi, bias: (b, qi, hp)),
            scratch_shapes=[pltpu.VMEM((2, t + 3 * MOBA_BLOCK, 2 * HEAD_DIM), BF16),
                            pltpu.VMEM((2, V_ROWS, t + 3 * MOBA_BLOCK), BF16),
                            pltpu.VMEM((2, nbp, HEAD_DIM), F32),
                            pltpu.VMEM((2, 2 * MOBA_BLOCK, MOBA_BLOCK), BF16),
                            pltpu.VMEM((2, 2, 2 * MOBA_BLOCK, MOBA_BLOCK), F32)],
        ),
        out_shape=jax.ShapeDtypeStruct((nbat, t, MOBA_W), BF16),
        compiler_params=_cparams("parallel", "parallel", "arbitrary"),
        name="moba_prompt",
    )(bias_flat, q_t, k_t, v_t, near_tab)


N_PAGE_BUF = 32


def _sample_select_kernel(pt_ref, q_ref, cache_ref, sel_ref, buf, sem, km_scr, *, layer, n_pages, pages_per_block, n_buf):
    b = pl.program_id(0)
    nblk = n_pages // pages_per_block
    page = cache_ref.shape[-1]

    def copy(p, slot):
        return pltpu.make_async_copy(cache_ref.at[layer, pt_ref[b, p]], buf.at[slot], sem.at[slot])

    for p in range(n_buf):
        copy(p, p).start()
    km_scr[...] = jnp.zeros(km_scr.shape, F32)

    def block(j, carry):
        tot = None
        for u in range(pages_per_block):
            p = j * pages_per_block + u
            slot = p % n_buf
            copy(p, slot).wait()
            x = buf[slot].reshape(MOBA_W, page)
            tot = x if tot is None else tot + x

            @pl.when(p + n_buf < n_pages)
            def _():
                copy(p + n_buf, slot).start()

        km_scr[pl.ds(j, 1), :] = jnp.sum(tot.T, axis=0, keepdims=True) * (1.0 / (pages_per_block * page))
        return carry

    lax.fori_loop(0, nblk, block, 0)

    km = km_scr[...]
    nrow = km.shape[0]
    head_sum = (lax.broadcasted_iota(jnp.int32, (MOBA_W, 128), 0) // HEAD_DIM
                == lax.broadcasted_iota(jnp.int32, (MOBA_W, 128), 1))
    head_sum = jnp.where(head_sum, 1.0, 0.0).astype(BF16)
    rown = lax.broadcasted_iota(jnp.int32, (nrow, 128), 0)
    rown_f = rown.astype(F32)
    sel_ref[0] = jnp.zeros(sel_ref.shape[1:], jnp.int32)
    for t in range(q_ref.shape[1]):
        gate = _rdot3(km * q_ref[0, t:t + 1, :], head_sum)
        g = jnp.where(rown < nblk, gate, -jnp.inf)
        for r in range(MOBA_TOPK):
            mx = jnp.max(g, axis=0, keepdims=True)
            idx = jnp.min(jnp.where(g == mx, rown_f, float(nrow)), axis=0, keepdims=True)
            sel_ref[0, 4 * t + r:4 * t + r + 1, :] = idx.astype(jnp.int32)
            g = jnp.where(rown_f == idx, -jnp.inf, g)


def _sample_select(page_table, q_tb, cache_t, layer):
    nbat, n_pages = page_table.shape
    n_t = q_tb.shape[1]
    page = cache_t.shape[-1]
    ppb = MOBA_BLOCK // page
    nblk = n_pages // ppb
    assert n_pages % ppb == 0
    n_buf = min(N_PAGE_BUF, n_pages)
    sel_rows = -(-4 * n_t // 8) * 8
    return pl.pallas_call(
        functools.partial(_sample_select_kernel, layer=layer, n_pages=n_pages, pages_per_block=ppb, n_buf=n_buf),
        grid_spec=pltpu.PrefetchScalarGridSpec(
            num_scalar_prefetch=1,
            grid=(nbat,),
            in_specs=[pl.BlockSpec((1, n_t, MOBA_W), lambda b, pt: (b, 0, 0)),
                      pl.BlockSpec(memory_space=pl.ANY)],
            out_specs=pl.BlockSpec((1, sel_rows, 128), lambda b, pt: (b, 0, 0)),
            scratch_shapes=[pltpu.VMEM((n_buf, H_MOBA, HEAD_DIM, page), F32),
                            pltpu.SemaphoreType.DMA((n_buf,)),
                            pltpu.VMEM((-(-nblk // 8) * 8, MOBA_W), F32)],
        ),
        out_shape=jax.ShapeDtypeStruct((nbat, sel_rows, 128), jnp.int32),
        compiler_params=_cparams("arbitrary"),
        name="sample_select",
    )(page_table, q_tb, cache_t)


N_HEAD_BUF = 4
N_AHEAD = 2


def _sample_attend_kernel(pid_ref, sel_ref, q_ref, kn_ref, vn_ref, bias_ref, ck_ref, cv_ref, o_ref,
                          kbuf, vbuf, sem, *, layer, n_t, pages_per_block, nblk):
    b = pl.program_id(0)
    nbat = pl.num_programs(0)
    page = ck_ref.shape[-1]
    n_slot = n_t * MOBA_TOPK

    def copies(bb, h, par):
        out = []
        for slot in range(n_slot):
            for u in range(pages_per_block):
                pid = pid_ref[bb, (h * n_slot + slot) * pages_per_block + u]
                lanes = pl.ds(u * page, page)
                out.append(pltpu.make_async_copy(ck_ref.at[layer, pid, h], kbuf.at[par, slot, :, lanes], sem.at[par, 0]))
                out.append(pltpu.make_async_copy(cv_ref.at[layer, pid, h], vbuf.at[par, slot, :, lanes], sem.at[par, 1]))
        return out

    @pl.when(b == 0)
    def _():
        for h0 in range(N_AHEAD):
            for cp in copies(b, h0, h0):
                cp.start()

    causal_own = lax.broadcasted_iota(jnp.int32, (n_t, n_t), 1) <= lax.broadcasted_iota(jnp.int32, (n_t, n_t), 0)
    for h in range(H_MOBA):
        par = h % N_HEAD_BUF
        nxt = h + N_AHEAD
        if nxt < H_MOBA:
            for cp in copies(b, nxt, nxt % N_HEAD_BUF):
                cp.start()
        else:
            @pl.when(b + 1 < nbat)
            def _():
                for cp in copies(b + 1, nxt - H_MOBA, nxt % N_HEAD_BUF):
                    cp.start()

        for cp in copies(b, h, par):
            cp.wait()
        rows = slice(h * HEAD_DIM, (h + 1) * HEAD_DIM)
        qh = q_ref[0, rows, :] * (HEAD_DIM ** -0.5)
        kn = kn_ref[0, rows, :]
        vn = vn_ref[0, rows, :]
        s_rows, own_rows = [], []
        for t in range(n_t):
            qb = jnp.broadcast_to(qh[:, t:t + 1], (HEAD_DIM, MOBA_BLOCK))
            own_rows.append(jnp.sum(qh[:, t:t + 1] * kn, axis=0, keepdims=True) + bias_ref[h, n_t + 1 + t:n_t + 2 + t, 0:n_t])
            for r in range(MOBA_TOPK):
                slot = t * MOBA_TOPK + r
                j = sel_ref[b, h * n_slot + slot]
                prod = qb * kbuf[par, slot]
                s = jnp.sum(prod.reshape(HEAD_DIM // 8, 8, MOBA_BLOCK).sum(axis=0), axis=0, keepdims=True)
                s_rows.append(s + jnp.where(j == nblk - 1, bias_ref[h, t:t + 1, :], bias_ref[h, n_t:n_t + 1, :]))
        s_all = jnp.concatenate(s_rows, axis=0)
        s_own = jnp.where(causal_own, jnp.concatenate(own_rows, axis=0), MASK_NEG)
        m_slot = jnp.max(s_all, axis=1, keepdims=True)
        m_t = jnp.max(s_own, axis=1, keepdims=True)
        m_rows = []
        for t in range(n_t):
            mt = m_t[t:t + 1]
            for r in range(MOBA_TOPK):
                mt = jnp.maximum(mt, m_slot[t * MOBA_TOPK + r:t * MOBA_TOPK + r + 1])
            m_rows.append(mt)
        m_t = jnp.concatenate(m_rows, axis=0)
        m_rep = jnp.concatenate([m_rows[t] for t in range(n_t) for _ in range(MOBA_TOPK)], axis=0)
        p_all = jnp.exp(s_all - m_rep)
        p_own = jnp.exp(s_own - m_t)
        l_slot = jnp.sum(p_all, axis=1, keepdims=True)
        l_own = jnp.sum(p_own, axis=1, keepdims=True)
        for t in range(n_t):
            wsum = None
            l = l_own[t:t + 1]
            for r in range(MOBA_TOPK):
                slot = t * MOBA_TOPK + r
                l = l + l_slot[slot:slot + 1]
                term = p_all[slot:slot + 1] * vbuf[par, slot]
                wsum = term if wsum is None else wsum + term
            acc = jnp.sum(wsum, axis=1, keepdims=True) + jnp.sum(p_own[t:t + 1] * vn, axis=1, keepdims=True)
            o_ref[0, rows, t:t + 1] = acc / l


def _sample_attend(pids, sel, q_bt, kn_bt, vn_bt, bias_rows, cache_kt, cache_vt, layer, nblk):
    nbat = pids.shape[0]
    n_t = q_bt.shape[-1]
    page = cache_kt.shape[-1]
    ppb = MOBA_BLOCK // page
    n_slot = n_t * MOBA_TOPK
    new = pl.BlockSpec((1, MOBA_W, n_t), lambda b, pd, sl: (b, 0, 0))
    return pl.pallas_call(
        functools.partial(_sample_attend_kernel, layer=layer, n_t=n_t, pages_per_block=ppb, nblk=nblk),
        grid_spec=pltpu.PrefetchScalarGridSpec(
            num_scalar_prefetch=2,
            grid=(nbat,),
            in_specs=[new, new, new,
                      pl.BlockSpec(bias_rows.shape, lambda b, pd, sl: (0, 0, 0)),
                      pl.BlockSpec(memory_space=pl.ANY), pl.BlockSpec(memory_space=pl.ANY)],
            out_specs=new,
            scratch_shapes=[pltpu.VMEM((N_HEAD_BUF, n_slot, HEAD_DIM, MOBA_BLOCK), F32),
                            pltpu.VMEM((N_HEAD_BUF, n_slot, HEAD_DIM, MOBA_BLOCK), F32),
                            pltpu.SemaphoreType.DMA((N_HEAD_BUF, 2))],
        ),
        out_shape=jax.ShapeDtypeStruct((nbat, MOBA_W, n_t), F32),
        compiler_params=_cparams("arbitrary"),
        name="sample_attend",
    )(pids, sel, q_bt, kn_bt, vn_bt, bias_rows, cache_kt, cache_vt)


def _mlp_kernel(x_ref, olin_ref, omoba_ref, wl_ref, wm_ref, gpost_ref, fpre_ref, wg_ref, wu_ref, wd_ref, fpost_ref, y_ref):
    mix = _dot(olin_ref[...], wl_ref[...]) + _dot(omoba_ref[...], wm_ref[...])
    x1 = x_ref[...] + _rms(mix, gpost_ref[...])
    h = _rms(x1, fpre_ref[...]).astype(BF16)
    gate = _dot(h, wg_ref[...])
    up = _dot(h, wu_ref[...])
    act = (gate / (1.0 + jnp.exp(-gate)) * up).astype(BF16)
    y_ref[...] = x1 + _rms(_dot(act, wd_ref[...]), fpost_ref[...])


def _mlp(x, olin, omoba, w_lin, w_moba, gpost, fpre, wg, wu, wd, fpost, tm):
    n, d = x.shape
    const = lambda a: pl.BlockSpec(a.shape, lambda i: (0,) * a.ndim, pipeline_mode=pl.Buffered(1))
    rows = lambda w: pl.BlockSpec((tm, w), lambda i: (i, 0))
    return pl.pallas_call(
        _mlp_kernel,
        grid=(n // tm,),
        in_specs=[rows(d), rows(olin.shape[1]), rows(omoba.shape[1]), const(w_lin), const(w_moba), const(gpost),
                  const(fpre), const(wg), const(wu), const(wd), const(fpost)],
        out_specs=rows(d),
        out_shape=jax.ShapeDtypeStruct((n, d), F32),
        compiler_params=_cparams("parallel"),
        name="mlp",
    )(x, olin, omoba, w_lin, w_moba, gpost, fpre, wg, wu, wd, fpost)


def _t5_bucket_np(dist):
    n = np.maximum(dist, 0)
    max_exact = N_BUCKETS // 2
    nf = np.maximum(n, 1).astype(np.float32)
    large = max_exact + (np.log(nf / np.float32(max_exact)) / np.float32(math.log(MAX_DISTANCE / max_exact))
                         * np.float32(N_BUCKETS - max_exact)).astype(np.int32)
    large = np.minimum(large, N_BUCKETS - 1)
    return np.where(n < max_exact, n, large).astype(np.int32)


def _rotary_tables(pos):
    half = LIN_KEY_DIM // 2
    inv = 1.0 / (ROPE_BASE ** jnp.linspace(0.0, 1.0, half, dtype=F32))
    ang = pos.astype(F32)[:, None] * inv[None, :]
    tile = lambda a: jnp.tile(a, (1, 2 * N_LIN_HEADS))
    return tile(jnp.cos(ang)), tile(jnp.sin(ang))


def _blockdiag_state(s):
    eye = jnp.eye(N_LIN_HEADS, dtype=s.dtype)
    return (s[:, :, :, None, :] * eye[None, :, None, :, None]).reshape(s.shape[0], LIN_QK_W, LIN_V_W)


def _head_states(sb):
    x = sb.reshape(sb.shape[0], N_LIN_HEADS, LIN_KEY_DIM, N_LIN_HEADS, HEAD_DIM)
    eye = jnp.eye(N_LIN_HEADS, dtype=sb.dtype)
    return jnp.sum(x * eye[None, :, None, :, None], axis=3)


def _row_tile(n, want):
    t = min(n, want)
    while n % t:
        t //= 2
    return t


def kernel(x_prompt, x_sample, cache_k, cache_v, state_ret, state_gla, page_table, w_in, w_out, gla_gate_w2, gla_gate_b, ret_norm_g, gla_norm_g, norm_attn_pre, norm_attn_post, norm_ffn_pre, norm_ffn_post, w_ffn_gate, w_ffn_up, w_ffn_down, rel_bias):
    depth = w_in.shape[0]
    nbp_, t_p, d_model = x_prompt.shape
    nbs, t_s, _ = x_sample.shape
    n_pages = page_table.shape[1]
    page = cache_k.shape[2]
    past_len = n_pages * page
    assert past_len % MOBA_BLOCK == 0 and t_s <= MOBA_BLOCK and t_p % MOBA_BLOCK == 0
    dff = w_ffn_gate.shape[2]

    cos_p, sin_p = _rotary_tables(jnp.arange(t_p, dtype=jnp.int32))
    cos_s, sin_s = _rotary_tables(past_len + jnp.arange(t_s, dtype=jnp.int32))
    off = np.arange(MOBA_BLOCK)
    dist_own = off[None, :] - off[:, None]
    bucket_near = np.concatenate([_t5_bucket_np(MOBA_BLOCK + dist_own),
                                  np.where(dist_own >= 0, _t5_bucket_np(dist_own), -1)], axis=0).astype(np.int32)
    bias_h = rel_bias.astype(F32).T
    bias_flat = bias_h.reshape(-1)
    near_tab = _near_bias_table(bias_flat, bucket_near)
    far = bias_h[:, N_BUCKETS - 1]
    prev_rows = bias_h[:, _t5_bucket_np(MOBA_BLOCK + np.arange(t_s)[:, None] - off[None, :])]
    far_row = jnp.broadcast_to(far[:, None, None], (H_MOBA, 1, MOBA_BLOCK))
    own_rows = bias_h[:, _t5_bucket_np(np.arange(t_s)[:, None] - off[None, :])]
    n_rows = 2 * t_s + 1
    pad_rows = -(-n_rows // 8) * 8 - n_rows
    bias_rows = jnp.concatenate([prev_rows, far_row, own_rows, jnp.zeros((H_MOBA, pad_rows, MOBA_BLOCK), F32)], axis=1)

    rdec = jnp.repeat(jnp.log(1.0 - 2.0 ** (-5.0 - jnp.arange(N_LIN_HEADS, dtype=F32))), LIN_KEY_DIM)[None, :]
    cache_kt = jnp.transpose(cache_k, (0, 1, 3, 4, 2))
    cache_vt = jnp.transpose(cache_v, (0, 1, 3, 4, 2))
    w_in_t = jnp.swapaxes(w_in, 1, 2)

    hp = x_prompt
    hs = x_sample.reshape(1, nbs * t_s, d_model)
    zero_state = jnp.zeros((nbp_, 2, LIN_QK_W, LIN_V_W), F32)
    tm_p = _row_tile(t_p, 512)
    tb_p = _row_tile(t_p, 256)
    c_p = math.gcd(t_p, REC_CHUNK)
    c_s = math.gcd(t_s, REC_CHUNK)
    outs = {k: [] for k in ("kp", "vp", "rp", "gp", "ks", "vs", "rs", "gs")}

    for l in range(depth):
        seg = w_in_t[l]
        wtok = jnp.concatenate([seg[:LIN_TOK_W + GLA_RANK], jnp.zeros((GA_PAD - GLA_RANK, d_model), F32)], axis=0).T.astype(BF16)
        m0 = LIN_TOK_W + GLA_RANK
        wq_t = seg[m0:m0 + MOBA_W].astype(BF16)
        wk_t = seg[m0 + MOBA_W:m0 + 2 * MOBA_W].astype(BF16)
        wv_t = seg[m0 + 2 * MOBA_W:m0 + 3 * MOBA_W].astype(BF16)
        w2p = jnp.concatenate([gla_gate_w2[l], jnp.zeros((GA_PAD - GLA_RANK, LIN_QK_W), F32)], axis=0).astype(BF16)
        gb = gla_gate_b[l][None, :]
        rgain = jnp.tile(ret_norm_g[l], N_LIN_HEADS)[None, :]
        ggain = jnp.tile(gla_norm_g[l], N_LIN_HEADS)[None, :]
        w_o = w_out[l].astype(BF16)
        w_o_lin, w_o_moba = w_o[:2 * LIN_V_W], w_o[2 * LIN_V_W:]
        wg, wu, wd = w_ffn_gate[l].astype(BF16), w_ffn_up[l].astype(BF16), w_ffn_down[l].astype(BF16)
        g_pre, g_post = norm_attn_pre[l][None, :], norm_attn_post[l][None, :]
        f_pre, f_post = norm_ffn_pre[l][None, :], norm_ffn_post[l][None, :]

        tok, q_t, k_t, v_t = _inproj(hp, g_pre, wtok, wq_t, wk_t, wv_t, tm_p)
        olin, sfin = _linrec(tok, cos_p, sin_p, w2p, gb, rdec, rgain, ggain, zero_state, tb_p, c_p)
        omoba = _moba_prompt(q_t, k_t, v_t, bias_flat, near_tab)
        hp = _mlp(hp.reshape(-1, d_model), olin.reshape(-1, 2 * LIN_V_W), omoba.reshape(-1, MOBA_W),
                  w_o_lin, w_o_moba, g_post, f_pre, wg, wu, wd, f_post, tm_p).reshape(nbp_, t_p, d_model)
        outs["kp"].append(jnp.transpose(k_t.reshape(nbp_, H_MOBA, HEAD_DIM, t_p), (0, 3, 1, 2)))
        outs["vp"].append(jnp.transpose(v_t.reshape(nbp_, H_MOBA, HEAD_DIM, t_p), (0, 3, 1, 2)))
        outs["rp"].append(_head_states(sfin[:, 0]))
        outs["gp"].append(_head_states(sfin[:, 1]))

        n_s = nbs * t_s
        tok, q_t, k_t, v_t = _inproj(hs, g_pre, wtok, wq_t, wk_t, wv_t, n_s)
        s0 = jnp.stack([_blockdiag_state(state_ret[l]), _blockdiag_state(state_gla[l])], axis=1)
        olin, sfin = _linrec(tok.reshape(nbs, t_s, TOK_W), cos_s, sin_s, w2p, gb, rdec, rgain, ggain, s0, t_s, c_s)
        per_batch = lambda a: jnp.transpose(a.reshape(MOBA_W, nbs, t_s), (1, 0, 2))
        q_bt, k_bt, v_bt = per_batch(q_t), per_batch(k_t), per_batch(v_t)
        sel = _sample_select(page_table, jnp.transpose(q_bt, (0, 2, 1)), cache_kt, l)
        sel = jnp.transpose(sel[:, :4 * t_s, :H_MOBA].reshape(nbs, t_s, 4, H_MOBA)[:, :, :MOBA_TOPK], (0, 3, 1, 2))
        sel = sel.reshape(nbs, -1)
        ppb = MOBA_BLOCK // page
        logical = (sel[:, :, None] * ppb + jnp.arange(ppb, dtype=jnp.int32)[None, None, :]).reshape(nbs, -1)
        hit = logical[:, :, None] == jnp.arange(n_pages, dtype=jnp.int32)[None, None, :]
        pids = jnp.sum(jnp.where(hit, page_table[:, None, :], 0), axis=-1)
        o_bt = _sample_attend(pids, sel, q_bt, k_bt, v_bt, bias_rows, cache_kt, cache_vt, l, n_pages // ppb)
        omoba = jnp.transpose(o_bt, (0, 2, 1)).reshape(n_s, MOBA_W).astype(BF16)
        hs = _mlp(hs.reshape(n_s, d_model), olin.reshape(n_s, 2 * LIN_V_W), omoba, w_o_lin, w_o_moba, g_post,
                  f_pre, wg, wu, wd, f_post, n_s).reshape(1, n_s, d_model)
        to_cache = lambda a: jnp.transpose(a.reshape(H_MOBA, HEAD_DIM, nbs, t_s), (2, 3, 0, 1))
        outs["ks"].append(to_cache(k_t))
        outs["vs"].append(to_cache(v_t))
        outs["rs"].append(_head_states(sfin[:, 0]))
        outs["gs"].append(_head_states(sfin[:, 1]))

    st = lambda k: jnp.stack(outs[k])
    return (hp, hs.reshape(nbs, t_s, d_model), st("kp"), st("vp"), st("rp"), st("gp"),
            st("ks"), st("vs"), st("rs"), st("gs"))
```
